```python
import jax, jax.numpy as jnp
from jax import lax
import numpy as np

D_MODEL = 1024
BATCH = 4
SEQ = 4096
DEPTH = 2
DEC_BATCH = 128
DEC_SEQ = 4
PAST_LEN = 2048
PAGE_SIZE = 128

N_HEADS = 8
HEAD_DIM = 64
D_ATTN = N_HEADS * HEAD_DIM
MOBA_BLOCK = 256
MOBA_TOPK = 3
D_CONV = 512
CONV_W = 31
D_FF = 2816
N_EXPERTS = 8
TOP_K = 2
D_FF_EXPERT = 2816
N_MOD = 6
N_DENSE = (DEPTH + 1) // 2
N_MOE = DEPTH // 2
D_IN = 3 * D_ATTN + 2 * D_CONV + 2 * D_MODEL
PROMPT_Q_CHUNK = 32
SAMPLE_Q_CHUNK = 1
NORM_EPS = 1e-6
LN_EPS = 1e-5

kernel_name = 'moba_conformer_gated_hybrid_step'


def rms_norm(x, g):
    xf = x.astype(jnp.float32)
    y = xf * lax.rsqrt(jnp.mean(xf * xf, axis=-1, keepdims=True) + NORM_EPS)
    return y.astype(x.dtype) * g


def layer_norm(x, g, b):
    xf = x.astype(jnp.float32)
    mu = jnp.mean(xf, axis=-1, keepdims=True)
    var = jnp.mean(jnp.square(xf - mu), axis=-1, keepdims=True)
    return ((xf - mu) * lax.rsqrt(var + LN_EPS)).astype(x.dtype) * g + b


def moba_attention(q, k_all, v_all, q_pos, q_chunk):
    b, n_q, h, dh = q.shape
    n_k = k_all.shape[1]
    n_blk = -(-n_k // MOBA_BLOCK)
    pad = n_blk * MOBA_BLOCK - n_k

    def to_blocks(t):
        t = jnp.pad(t, ((0, 0), (0, pad), (0, 0), (0, 0)))
        return t.reshape(b, n_blk, MOBA_BLOCK, h, dh).transpose(0, 3, 1, 2, 4)

    kb = to_blocks(k_all)
    vb = to_blocks(v_all)
    k_mean = jnp.mean(kb.astype(jnp.float32), axis=3)
    n_sel = min(MOBA_TOPK, n_blk)
    n_c = n_q // q_chunk
    q_c = q.reshape(b, n_c, q_chunk, h, dh).transpose(1, 0, 3, 2, 4)
    pos_c = q_pos.reshape(n_c, q_chunk)
    b_idx = jnp.arange(b)[:, None, None, None]
    h_idx = jnp.arange(h)[None, :, None, None]
    blk_ids = jnp.arange(n_blk)
    in_blk = jnp.arange(MOBA_BLOCK)
    scale = HEAD_DIM ** -0.5

    def attend(args):
        qb, pb = args
        own = pb // MOBA_BLOCK
        gate = jnp.einsum('bhqd,bhnd->bhqn', qb.astype(jnp.float32), k_mean)
        gate = jnp.where(blk_ids[None, :] < own[:, None], gate, -jnp.inf)
        _, top_idx = lax.top_k(gate, n_sel)
        own_b = jnp.broadcast_to(own[None, None, :, None], (b, h, q_chunk, 1))
        slots = jnp.concatenate([top_idx, own_b], axis=-1)
        slot_ok = jnp.concatenate([top_idx < own[:, None], jnp.ones(own_b.shape, dtype=bool)], axis=-1)
        k_sel = kb[b_idx, h_idx, slots]
        v_sel = vb[b_idx, h_idx, slots]
        key_pos = slots[..., None] * MOBA_BLOCK + in_blk
        mask = slot_ok[..., None] & (key_pos <= pb[:, None, None])
        s = jnp.einsum('bhqd,bhqskd->bhqsk', qb, k_sel).astype(jnp.float32) * scale
        s = jnp.where(mask, s, -jnp.inf).reshape(b, h, q_chunk, -1)
        p = jax.nn.softmax(s, axis=-1).reshape(mask.shape).astype(v_sel.dtype)
        return jnp.einsum('bhqsk,bhqskd->bhqd', p, v_sel)

    out = lax.map(attend, (q_c, pos_c))
    return out.transpose(1, 0, 3, 2, 4).reshape(b, n_q, h, dh)


def causal_depthwise_conv(u, ctx, w, bias):
    full = jnp.concatenate([ctx.astype(u.dtype), u], axis=1)
    y = lax.conv_general_dilated(full, w[:, None, :].astype(u.dtype), window_strides=(1,), padding='VALID',
                                 dimension_numbers=('NWC', 'WIO', 'NWC'), feature_group_count=u.shape[-1])
    return y + bias, full[:, -(CONV_W - 1):]


def swiglu(t, w1, w3, w2):
    return (jax.nn.silu(t @ w1) * (t @ w3)) @ w2


def moe_swiglu(h, router_w, router_b, w1, w3, w2):
    shp = h.shape
    t = h.reshape(-1, shp[-1])
    logits = (t @ router_w).astype(jnp.float32) + router_b.astype(jnp.float32)
    top_val, top_idx = lax.top_k(logits, TOP_K)
    top_w = jax.nn.softmax(top_val, axis=-1)
    gate = jnp.sum(jax.nn.one_hot(top_idx, N_EXPERTS, dtype=jnp.float32) * top_w[..., None], axis=1).astype(t.dtype)
    y = jnp.zeros_like(t)
    for e in range(N_EXPERTS):
        y = y + gate[:, e:e + 1] * swiglu(t, w1[e], w3[e], w2[e])
    return y.reshape(shp)


def setup_inputs(seed: int = 0) -> dict:
    key = jax.random.key(seed)
    keys = jax.random.split(key, 40)
    counter = [0]

    def nk():
        counter[0] += 1
        return keys[counter[0] - 1]

    def nrm(shape, scale):
        return jax.random.normal(nk(), shape, jnp.float32) * scale

    def gain(shape):
        return 1.0 + nrm(shape, 0.02)

    n_pages = PAST_LEN // PAGE_SIZE
    n_phys = (5 * DEC_BATCH * n_pages) // 4
    x_prompt = nrm((BATCH, SEQ, D_MODEL), 1.0)
    x_sample = nrm((DEC_BATCH, DEC_SEQ, D_MODEL), 1.0)
    cache_k = nrm((DEPTH, n_phys, PAGE_SIZE, N_HEADS, HEAD_DIM), 1.0)
    cache_v = nrm((DEPTH, n_phys, PAGE_SIZE, N_HEADS, HEAD_DIM), 1.0)
    state_conv = nrm((DEPTH, DEC_BATCH, CONV_W - 1, D_CONV), 0.5)
    page_table = jax.random.permutation(nk(), n_phys)[:DEC_BATCH * n_pages].reshape(DEC_BATCH, n_pages).astype(jnp.int32)
    c_prompt = nrm((BATCH, D_MODEL), 1.0)
    c_sample = nrm((DEC_BATCH, D_MODEL), 1.0)
    return {
        'x_prompt': x_prompt,
        'x_sample': x_sample,
        'cache_k': cache_k,
        'cache_v': cache_v,
        'state_conv': state_conv,
        'page_table': page_table,
        'c_prompt': c_prompt,
        'c_sample': c_sample,
        'w_mod': nrm((DEPTH, D_MODEL, N_MOD * D_MODEL), 0.5 * D_MODEL ** -0.5),
        'b_mod': nrm((DEPTH, N_MOD * D_MODEL), 0.02),
        'norm1_g': gain((DEPTH, D_MODEL)),
        'norm2_g': gain((DEPTH, D_MODEL)),
        'w_in': nrm((DEPTH, D_MODEL, D_IN), D_MODEL ** -0.5),
        'q_norm_g': gain((DEPTH, HEAD_DIM)),
        'k_norm_g': gain((DEPTH, HEAD_DIM)),
        'w_o_attn': nrm((DEPTH, D_ATTN, D_MODEL), D_ATTN ** -0.5),
        'w_dw': nrm((DEPTH, CONV_W, D_CONV), CONV_W ** -0.5),
        'b_dw': nrm((DEPTH, D_CONV), 0.02),
        'conv_ln_g': gain((DEPTH, D_CONV)),
        'conv_ln_b': nrm((DEPTH, D_CONV), 0.02),
        'w_pw_conv': nrm((DEPTH, D_CONV, D_MODEL), D_CONV ** -0.5),
        'w_out': nrm((DEPTH, D_MODEL, D_MODEL), D_MODEL ** -0.5),
        'ffn_w1': nrm((N_DENSE, D_MODEL, D_FF), D_MODEL ** -0.5),
        'ffn_w3': nrm((N_DENSE, D_MODEL, D_FF), D_MODEL ** -0.5),
        'ffn_w2': nrm((N_DENSE, D_FF, D_MODEL), D_FF ** -0.5),
        'router_w': nrm((N_MOE, D_MODEL, N_EXPERTS), D_MODEL ** -0.5),
        'router_b': nrm((N_MOE, N_EXPERTS), 0.01),
        'moe_w1': nrm((N_MOE, N_EXPERTS, D_MODEL, D_FF_EXPERT), D_MODEL ** -0.5),
        'moe_w3': nrm((N_MOE, N_EXPERTS, D_MODEL, D_FF_EXPERT), D_MODEL ** -0.5),
        'moe_w2': nrm((N_MOE, N_EXPERTS, D_FF_EXPERT, D_MODEL), D_FF_EXPERT ** -0.5),
    }


def reference(x_prompt, x_sample, cache_k, cache_v, state_conv, page_table, c_prompt, c_sample,
              w_mod, b_mod, norm1_g, norm2_g, w_in, q_norm_g, k_norm_g, w_o_attn, w_dw, b_dw,
              conv_ln_g, conv_ln_b, w_pw_conv, w_out, ffn_w1, ffn_w3, ffn_w2,
              router_w, router_b, moe_w1, moe_w3, moe_w2):
    bounds = (D_ATTN, 2 * D_ATTN, 3 * D_ATTN, 3 * D_ATTN + D_CONV, 3 * D_ATTN + 2 * D_CONV,
              3 * D_ATTN + 2 * D_CONV + D_MODEL)

    def trunk(x, c, past_len, q_chunk, get_past):
        n_b, n_t, _ = x.shape
        cond = jax.nn.silu(c)
        q_pos = past_len + jnp.arange(n_t, dtype=jnp.int32)
        new_k, new_v, new_conv = [], [], []
        for l in range(DEPTH):
            mod = (cond @ w_mod[l] + b_mod[l]).reshape(n_b, N_MOD, 1, D_MODEL)
            shift_m, scale_m, gate_m, shift_f, scale_f, gate_f = (mod[:, i] for i in range(N_MOD))
            past_k, past_v, conv_ctx = get_past(l)
            h = rms_norm(x, norm1_g[l]) * (1 + scale_m) + shift_m
            q, k, v, glu_a, glu_b, gate_attn, gate_conv = jnp.split(h @ w_in[l], bounds, axis=-1)
            q = rms_norm(q.reshape(n_b, n_t, N_HEADS, HEAD_DIM), q_norm_g[l])
            k = rms_norm(k.reshape(n_b, n_t, N_HEADS, HEAD_DIM), k_norm_g[l])
            v = v.reshape(n_b, n_t, N_HEADS, HEAD_DIM)
            k_all = jnp.concatenate([past_k.astype(k.dtype), k], axis=1)
            v_all = jnp.concatenate([past_v.astype(v.dtype), v], axis=1)
            attn = moba_attention(q, k_all, v_all, q_pos, q_chunk)
            y_attn = attn.reshape(n_b, n_t, D_ATTN) @ w_o_attn[l]
            u = glu_a * jax.nn.sigmoid(glu_b)
            u, conv_state = causal_depthwise_conv(u, conv_ctx, w_dw[l], b_dw[l])
            y_conv = jax.nn.silu(layer_norm(u, conv_ln_g[l], conv_ln_b[l])) @ w_pw_conv[l]
            merged = jax.nn.sigmoid(gate_attn) * y_attn + jax.nn.sigmoid(gate_conv) * y_conv
            x = x + gate_m * (merged @ w_out[l])
            h = rms_norm(x, norm2_g[l]) * (1 + scale_f) + shift_f
            if l % 2 == 0:
                f = swiglu(h, ffn_w1[l // 2], ffn_w3[l // 2], ffn_w2[l // 2])
            else:
                f = moe_swiglu(h, router_w[l // 2], router_b[l // 2], moe_w1[l // 2], moe_w3[l // 2], moe_w2[l // 2])
            x = x + gate_f * f
            new_k.append(k)
            new_v.append(v)
            new_conv.append(conv_state)
        return x, jnp.stack(new_k), jnp.stack(new_v), jnp.stack(new_conv)

    n_prompt = x_prompt.shape[0]
    n_samp = x_sample.shape[0]
    past_len = page_table.shape[1] * PAGE_SIZE

    def prompt_past(l):
        empty = jnp.zeros((n_prompt, 0, N_HEADS, HEAD_DIM), x_prompt.dtype)
        return empty, empty, jnp.zeros((n_prompt, CONV_W - 1, D_CONV), x_prompt.dtype)

    def sample_past(l):
        pk = cache_k[l, page_table].reshape(n_samp, past_len, N_HEADS, HEAD_DIM)
        pv = cache_v[l, page_table].reshape(n_samp, past_len, N_HEADS, HEAD_DIM)
        return pk, pv, state_conv[l]

    y_prompt, k_prompt, v_prompt, conv_prompt = trunk(x_prompt, c_prompt, 0, PROMPT_Q_CHUNK, prompt_past)
    y_sample, k_sample, v_sample, conv_sample = trunk(x_sample, c_sample, past_len, SAMPLE_Q_CHUNK, sample_past)
    return (y_prompt, y_sample, k_prompt, v_prompt, conv_prompt, k_sample, v_sample, conv_sample)
```

```python
import functools

import jax
import jax.numpy as jnp
from jax import lax
from jax.experimental import pallas as pl
from jax.experimental.pallas import tpu as pltpu

F32 = jnp.float32
BF16 = jnp.bfloat16
HIGHEST = lax.Precision.HIGHEST

D_MODEL = 1024
N_HEADS = 8
HEAD_DIM = 64
D_ATTN = N_HEADS * HEAD_DIM
MOBA_BLOCK = 256
MOBA_TOPK = 3
D_CONV = 512
CONV_W = 31
N_EXPERTS = 8
N_MOD = 6
PAGE_SIZE = 128
NORM_EPS = 1e-6
LN_EPS = 1e-5
NEG = -1e30
CONV_HALO = 32
VMEM_LIMIT = 56 * 1024 * 1024


def _cparams(sem):
    return pltpu.CompilerParams(dimension_semantics=sem, vmem_limit_bytes=VMEM_LIMIT)


def _dot(a, b):
    return jnp.dot(a, b, preferred_element_type=F32)


def _dot_nt(a, b, precision=None):
    return lax.dot_general(a, b, (((1,), (1,)), ((), ())), precision=precision, preferred_element_type=F32)


def _dot_tn(a, b):
    return lax.dot_general(a, b, (((0,), (0,)), ((), ())), preferred_element_type=F32)


def _sigmoid(x):
    return 1.0 / (1.0 + jnp.exp(-x))


def _top_select(score, valid, idx, n_sel, axis, sentinel):
    remaining = valid
    sel = jnp.zeros(score.shape, dtype=jnp.bool_)
    for _ in range(n_sel):
        cur = jnp.where(remaining, score, -jnp.inf)
        best = jnp.max(cur, axis=axis, keepdims=True)
        cand = remaining & (cur == best)
        first = jnp.min(jnp.where(cand, idx, sentinel), axis=axis, keepdims=True)
        pick = idx == first
        sel = sel | pick
        remaining = remaining & jnp.logical_not(pick)
    return sel


def _mod_kernel(c_ref, w_ref, b_ref, o_ref):
    c = c_ref[...]
    cond = c * _sigmoid(c)
    o_ref[0] = _dot(cond.astype(BF16), w_ref[0].astype(BF16)) + b_ref[0]


def _modulation(c_all, w_mod, b_mod):
    depth, d, n = w_mod.shape
    rows = c_all.shape[0]
    tn = 1536
    return pl.pallas_call(
        _mod_kernel,
        grid=(depth, n // tn),
        in_specs=[
            pl.BlockSpec((rows, d), lambda l, j: (0, 0)),
            pl.BlockSpec((1, d, tn), lambda l, j: (l, 0, j)),
            pl.BlockSpec((1, 1, tn), lambda l, j: (l, 0, j)),
        ],
        out_specs=pl.BlockSpec((1, rows, tn), lambda l, j: (l, 0, j)),
        out_shape=jax.ShapeDtypeStruct((depth, rows, n), F32),
        compiler_params=_cparams(("parallel", "parallel")),
        name="adaln_modulation",
    )(c_all, w_mod, b_mod.reshape(depth, 1, n))


def _inproj_kernel(x_ref, shift_ref, scale_ref, g1_ref, w_ref, hm_ref, gq_ref, gk_ref,
                   q_ref, kh_ref, vh_ref, kf_ref, vf_ref, u_ref, ga_ref, gc_ref, km_ref, *, tm):
    x = x_ref[0]
    ms = jnp.mean(x * x, axis=-1, keepdims=True)
    h = (x * lax.rsqrt(ms + NORM_EPS)) * g1_ref[...]
    h = h * (1.0 + scale_ref[0]) + shift_ref[0]
    hb = h.astype(BF16)

    def proj(lo, hi):
        return _dot(hb, w_ref[:, lo:hi])

    def head_rms(z, g_ref):
        z2 = z * z
        z2_hi = z2.astype(BF16)
        z2_lo = (z2 - z2_hi.astype(F32)).astype(BF16)
        msh = _dot(z2_hi, hm_ref[...]) + _dot(z2_lo, hm_ref[...])
        return z * lax.rsqrt(msh + NORM_EPS) * g_ref[...]

    b0, b1, b2, b3, b4, b5, b6 = (0, D_ATTN, 2 * D_ATTN, 3 * D_ATTN, 3 * D_ATTN + D_CONV,
                                  3 * D_ATTN + 2 * D_CONV, 3 * D_ATTN + 2 * D_CONV + D_MODEL)
    q = head_rms(proj(b0, b1), gq_ref)
    k = head_rms(proj(b1, b2), gk_ref)
    v = proj(b2, b3)
    kf_ref[0] = k
    vf_ref[0] = v
    kb = k.astype(BF16)
    vb = v.astype(BF16)
    for hh in range(N_HEADS):
        sl = slice(hh * HEAD_DIM, (hh + 1) * HEAD_DIM)
        q_ref[0, hh] = q[:, sl]
        kh_ref[0, hh] = kb[:, sl]
        vh_ref[0, hh] = vb[:, sl]
    for bi in range(tm // MOBA_BLOCK):
        km_ref[0, bi] = jnp.mean(k[bi * MOBA_BLOCK:(bi + 1) * MOBA_BLOCK], axis=0, keepdims=True)
    u_ref[0] = proj(b3, b4) * _sigmoid(proj(b4, b5))
    ga_ref[0] = _sigmoid(proj(b5, b6)).astype(BF16)
    gc_ref[0] = _sigmoid(proj(b6, b6 + D_MODEL)).astype(BF16)


def _in_projection(x, shift, scale, g1, w_in_b, head_mean, gq, gk, *, tm):
    g, tg, d = x.shape
    r = shift.shape[1]
    n_in = w_in_b.shape[1]
    if r == 1:
        mod_spec = pl.BlockSpec((1, 1, d), lambda b, i: (b, 0, 0))
    else:
        mod_spec = pl.BlockSpec((1, tm, d), lambda b, i: (b, i, 0))
    const2 = lambda b, i: (0, 0)
    tok = lambda w: pl.BlockSpec((1, tm, w), lambda b, i: (b, i, 0))
    heads = pl.BlockSpec((1, N_HEADS, tm, HEAD_DIM), lambda b, i: (b, 0, i, 0))
    nb = tm // MOBA_BLOCK
    out_shapes = (
        jax.ShapeDtypeStruct((g, N_HEADS, tg, HEAD_DIM), F32),
        jax.ShapeDtypeStruct((g, N_HEADS, tg, HEAD_DIM), BF16),
        jax.ShapeDtypeStruct((g, N_HEADS, tg, HEAD_DIM), BF16),
        jax.ShapeDtypeStruct((g, tg, D_ATTN), F32),
        jax.ShapeDtypeStruct((g, tg, D_ATTN), F32),
        jax.ShapeDtypeStruct((g, tg, D_CONV), F32),
        jax.ShapeDtypeStruct((g, tg, d), BF16),
        jax.ShapeDtypeStruct((g, tg, d), BF16),
        jax.ShapeDtypeStruct((g, tg // MOBA_BLOCK, 1, D_ATTN), F32),
    )
    out_specs = (heads, heads, heads, tok(D_ATTN), tok(D_ATTN), tok(D_CONV), tok(d), tok(d),
                 pl.BlockSpec((1, nb, 1, D_ATTN), lambda b, i: (b, i, 0, 0)))
    return pl.pallas_call(
        functools.partial(_inproj_kernel, tm=tm),
        grid=(g, tg // tm),
        in_specs=[tok(d), mod_spec, mod_spec,
                  pl.BlockSpec((1, d), const2),
                  pl.BlockSpec((d, n_in), const2),
                  pl.BlockSpec((D_ATTN, D_ATTN), const2),
                  pl.BlockSpec((1, D_ATTN), const2),
                  pl.BlockSpec((1, D_ATTN), const2)],
        out_specs=out_specs,
        out_shape=out_shapes,
        compiler_params=_cparams(("parallel", "parallel")),
        name="in_projection",
    )(x, shift, scale, g1, w_in_b, head_mean, gq, gk)


def _moba_prompt_kernel(q_ref, k_ref, v_ref, km_ref, o_ref, *, n_blk):
    j = pl.program_id(2)
    tq = MOBA_BLOCK
    q = q_ref[0, 0]
    gate = _dot_nt(q, km_ref[0, 0], precision=HIGHEST)
    blk = lax.broadcasted_iota(jnp.int32, (tq, n_blk), 1)
    sel = _top_select(gate, blk < j, blk, min(MOBA_TOPK, n_blk), 1, n_blk)
    sel_f = jnp.where(sel, 1.0, 0.0)
    qb = (q * (HEAD_DIM ** -0.5)).astype(BF16)

    start = pl.multiple_of(j * tq, tq)
    s = _dot_nt(qb, k_ref[0, 0, pl.ds(start, tq), :])
    row = lax.broadcasted_iota(jnp.int32, (tq, tq), 0)
    col = lax.broadcasted_iota(jnp.int32, (tq, tq), 1)
    s = jnp.where(col <= row, s, NEG)
    m0 = jnp.max(s, axis=-1, keepdims=True)
    p = jnp.exp(s - m0)
    l0 = jnp.sum(p, axis=-1, keepdims=True)
    acc0 = _dot(p.astype(BF16), v_ref[0, 0, pl.ds(start, tq), :])

    def body(i, carry):
        m, l, acc = carry
        st = pl.multiple_of(i * tq, tq)
        s = _dot_nt(qb, k_ref[0, 0, pl.ds(st, tq), :])
        chosen = jnp.max(jnp.where(blk == i, sel_f, 0.0), axis=-1, keepdims=True)
        s = jnp.where(chosen > 0.0, s, NEG)
        m_new = jnp.maximum(m, jnp.max(s, axis=-1, keepdims=True))
        alpha = jnp.exp(m - m_new)
        p = jnp.exp(s - m_new)
        l = alpha * l + jnp.sum(p, axis=-1, keepdims=True)
        acc = alpha * acc + _dot(p.astype(BF16), v_ref[0, 0, pl.ds(st, tq), :])
        return m_new, l, acc

    _, l, acc = lax.fori_loop(0, j, body, (m0, l0, acc0))
    o_ref[0, 0] = (acc / l).astype(o_ref.dtype)


def _moba_prompt(q, kh, vh, kmean):
    b, h, t, dh = q.shape
    n_blk = t // MOBA_BLOCK
    qspec = pl.BlockSpec((1, 1, MOBA_BLOCK, dh), lambda bi, hi, j: (bi, hi, j, 0))
    kvspec = pl.BlockSpec((1, 1, t, dh), lambda bi, hi, j: (bi, hi, 0, 0))
    return pl.pallas_call(
        functools.partial(_moba_prompt_kernel, n_blk=n_blk),
        grid=(b, h, n_blk),
        in_specs=[qspec, kvspec, kvspec,
                  pl.BlockSpec((1, 1, n_blk, dh), lambda bi, hi, j: (bi, hi, 0, 0))],
        out_specs=qspec,
        out_shape=jax.ShapeDtypeStruct((b, h, t, dh), BF16),
        compiler_params=_cparams(("parallel", "parallel", "arbitrary")),
        name="moba_prompt_attention",
    )(q, kh, vh, kmean)


def _moba_sample_kernel(pt_ref, qall_ref, knew_ref, vnew_ref, *refs, n_pages, n_q):
    del pt_ref
    k_refs = refs[:n_pages]
    v_refs = refs[n_pages:2 * n_pages]
    o_ref = refs[2 * n_pages]
    rows = PAGE_SIZE * N_HEADS
    n_col = N_HEADS * n_q
    pages_per_blk = MOBA_BLOCK // PAGE_SIZE
    n_blk = n_pages // pages_per_blk
    q_shift = n_q.bit_length() - 1
    h_shift = N_HEADS.bit_length() - 1

    qa = qall_ref[0]
    qb = (qa * (HEAD_DIM ** -0.5)).astype(BF16)

    def head_match(n_rows):
        r = lax.broadcasted_iota(jnp.int32, (n_rows, n_col), 0)
        c = lax.broadcasted_iota(jnp.int32, (n_rows, n_col), 1)
        return (r & (N_HEADS - 1)) == (c >> q_shift), r, c

    means = []
    for bi in range(n_blk):
        tot = jnp.zeros((N_HEADS, HEAD_DIM), F32)
        for pp in range(pages_per_blk):
            kp = k_refs[bi * pages_per_blk + pp][...]
            tot = tot + jnp.sum(kp.reshape(PAGE_SIZE, N_HEADS, HEAD_DIM), axis=0)
        means.append(tot * (1.0 / MOBA_BLOCK))
    kmean = jnp.concatenate(means, axis=0)
    g_all = _dot_nt(kmean, qa, precision=HIGHEST)
    match_g, _, _ = head_match(n_blk * N_HEADS)
    gate = jnp.sum(jnp.where(match_g, g_all, 0.0).reshape(n_blk, N_HEADS, n_col), axis=1)
    bidx = lax.broadcasted_iota(jnp.int32, (n_blk, n_col), 0)
    sel = _top_select(gate, bidx >= 0, bidx, min(MOBA_TOPK, n_blk + 1), 0, n_blk)
    sel_f = jnp.where(sel, 1.0, 0.0)

    match_n, r_n, c_n = head_match(n_q * N_HEADS)
    s = _dot_nt(knew_ref[0].astype(BF16), qb)
    s = jnp.where(match_n & ((r_n >> h_shift) <= (c_n & (n_q - 1))), s, NEG)
    m = jnp.max(s, axis=0, keepdims=True)
    p = jnp.exp(s - m)
    l = jnp.sum(p, axis=0, keepdims=True)
    acc = _dot_tn(vnew_ref[0].astype(BF16), p.astype(BF16))

    match_p, _, _ = head_match(rows)
    for pg in range(n_pages):
        bi = pg // pages_per_blk
        s = _dot_nt(k_refs[pg][...].astype(BF16), qb)
        s = jnp.where(match_p & (sel_f[bi:bi + 1, :] > 0.0), s, NEG)
        m_new = jnp.maximum(m, jnp.max(s, axis=0, keepdims=True))
        alpha = jnp.exp(m - m_new)
        p = jnp.exp(s - m_new)
        l = alpha * l + jnp.sum(p, axis=0, keepdims=True)
        acc = alpha * acc + _dot_tn(v_refs[pg][...].astype(BF16), p.astype(BF16))
        m = m_new
    o_ref[0] = acc / l


def _moba_sample(page_table, q_all, k_new, v_new, cache_k4, cache_v4, layer):
    n_seq, n_pages = page_table.shape
    n_col = q_all.shape[1]
    n_q = n_col // N_HEADS
    rows = PAGE_SIZE * N_HEADS
    seq3 = lambda shp: pl.BlockSpec((1,) + shp, lambda b, pt: (b, 0, 0))

    def page_spec(pg):
        return pl.BlockSpec((None, None, rows, HEAD_DIM), lambda b, pt, pg=pg: (layer, pt[b, pg], 0, 0))

    in_specs = [seq3((n_col, HEAD_DIM)), seq3((n_q * N_HEADS, HEAD_DIM)), seq3((n_q * N_HEADS, HEAD_DIM))]
    in_specs += [page_spec(pg) for pg in range(n_pages)]
    in_specs += [page_spec(pg) for pg in range(n_pages)]
    grid_spec = pltpu.PrefetchScalarGridSpec(
        num_scalar_prefetch=1, grid=(n_seq,), in_specs=in_specs,
        out_specs=pl.BlockSpec((1, HEAD_DIM, n_col), lambda b, pt: (b, 0, 0)))
    return pl.pallas_call(
        functools.partial(_moba_sample_kernel, n_pages=n_pages, n_q=n_q),
        grid_spec=grid_spec,
        out_shape=jax.ShapeDtypeStruct((n_seq, HEAD_DIM, n_col), F32),
        compiler_params=_cparams(("arbitrary",)),
        name="moba_sample_attention",
    )(page_table, q_all, k_new, v_new, *([cache_k4] * n_pages), *([cache_v4] * n_pages))


def _conv_sample_kernel(st_ref, u_ref, wst_ref, wu_ref, b_ref, o_ref, *, n_q):
    st = st_ref[...]
    u = u_ref[...]
    for t in range(n_q):
        y = jnp.sum(st * wst_ref[t], axis=1) + jnp.sum(u * wu_ref[t], axis=1)
        o_ref[t] = y + b_ref[...]


def _conv_sample(state, u, w_dw, b_dw):
    n_seq, ctx, c = state.shape
    n_q = u.shape[1]
    r = jnp.arange(ctx)[None, :] - jnp.arange(n_q)[:, None]
    w_state = jnp.where((r >= 0)[..., None], w_dw[jnp.clip(r, 0, CONV_W - 1)], 0.0)
    ju = ctx + jnp.arange(n_q)[None, :] - jnp.arange(n_q)[:, None]
    w_new = jnp.where((ju <= ctx)[..., None], w_dw[jnp.clip(ju, 0, CONV_W - 1)], 0.0)
    sb = 16
    out = pl.pallas_call(
        functools.partial(_conv_sample_kernel, n_q=n_q),
        grid=(n_seq // sb,),
        in_specs=[pl.BlockSpec((sb, ctx, c), lambda i: (i, 0, 0)),
                  pl.BlockSpec((sb, n_q, c), lambda i: (i, 0, 0)),
                  pl.BlockSpec((n_q, ctx, c), lambda i: (0, 0, 0)),
                  pl.BlockSpec((n_q, n_q, c), lambda i: (0, 0, 0)),
                  pl.BlockSpec((1, c), lambda i: (0, 0))],
        out_specs=pl.BlockSpec((n_q, sb, c), lambda i: (0, i, 0)),
        out_shape=jax.ShapeDtypeStruct((n_q, n_seq, c), F32),
        compiler_params=_cparams(("parallel",)),
        name="conv_sample",
    )(state, u, w_state, w_new, b_dw.reshape(1, c))
    return out.transpose(1, 0, 2)


def _post_kernel(*refs, tm, do_conv, do_route):
    it = iter(refs)
    x_ref, u_ref = next(it), next(it)
    halo_ref = next(it) if do_conv else None
    attn_ref, ga_ref, gc_ref, gm_ref, sf_ref, scf_ref = (next(it) for _ in range(6))
    if do_conv:
        wdw_ref, bdw_ref = next(it), next(it)
    lng_ref, lnb_ref, wpw_ref, wo_ref, wout_ref, g2_ref = (next(it) for _ in range(6))
    if do_route:
        rw_ref, rb_ref = next(it), next(it)
    xo_ref, h2_ref = next(it), next(it)
    gate_ref = next(it) if do_route else None
    if do_conv:
        xbuf, ybuf = next(it), next(it)

    if do_conv:
        i = pl.program_id(1)
        xbuf[0:CONV_HALO, :] = jnp.where(i > 0, halo_ref[0], 0.0)
        xbuf[CONV_HALO:CONV_HALO + tm, :] = u_ref[0]
        off = CONV_HALO - (CONV_W - 1)
        lane_chunk = 128
        for c0 in range(0, D_CONV, lane_chunk):
            acc = jnp.broadcast_to(bdw_ref[:, c0:c0 + lane_chunk], (tm, lane_chunk))
            for w in range(CONV_W):
                acc = acc + xbuf[off + w:off + w + tm, c0:c0 + lane_chunk] * wdw_ref[w:w + 1, c0:c0 + lane_chunk]
            ybuf[:, c0:c0 + lane_chunk] = acc
        yc = ybuf[...]
    else:
        yc = u_ref[0]

    mu = jnp.mean(yc, axis=-1, keepdims=True)
    var = jnp.mean(jnp.square(yc - mu), axis=-1, keepdims=True)
    yn = (yc - mu) * lax.rsqrt(var + LN_EPS) * lng_ref[...] + lnb_ref[...]
    y_conv = _dot((yn * _sigmoid(yn)).astype(BF16), wpw_ref[...])
    y_attn = _dot(attn_ref[0, 0], wo_ref[0])
    for hh in range(1, N_HEADS):
        y_attn = y_attn + _dot(attn_ref[0, hh], wo_ref[hh])
    merged = ga_ref[0].astype(F32) * y_attn + gc_ref[0].astype(F32) * y_conv
    xn = x_ref[0] + gm_ref[0] * _dot(merged.astype(BF16), wout_ref[...])
    xo_ref[0] = xn
    ms = jnp.mean(xn * xn, axis=-1, keepdims=True)
    h2 = (xn * lax.rsqrt(ms + NORM_EPS)) * g2_ref[...]
    h2 = h2 * (1.0 + scf_ref[0]) + sf_ref[0]
    h2_ref[0] = h2.astype(BF16)
    if do_route:
        logits = jnp.dot(h2, rw_ref[...], precision=HIGHEST, preferred_element_type=F32) + rb_ref[...]
        eidx = lax.broadcasted_iota(jnp.int32, logits.shape, 1)
        top1 = jnp.max(logits, axis=-1, keepdims=True)
        i1 = jnp.min(jnp.where(logits == top1, eidx, N_EXPERTS), axis=-1, keepdims=True)
        rest = jnp.where(eidx == i1, -jnp.inf, logits)
        top2 = jnp.max(rest, axis=-1, keepdims=True)
        i2 = jnp.min(jnp.where(rest == top2, eidx, N_EXPERTS), axis=-1, keepdims=True)
        e2 = jnp.exp(top2 - top1)
        w1 = 1.0 / (1.0 + e2)
        w2 = e2 / (1.0 + e2)
        gate_ref[0] = jnp.where(eidx == i1, w1, 0.0) + jnp.where(eidx == i2, w2, 0.0)


def _post_mixer(x, u, attn, ga, gc, gate_m, shift_f, scale_f, conv_w, ln_g, ln_b, w_pw_b, w_o_b, w_out_b, g2,
                router, *, tm):
    g, tg, d = x.shape
    r = gate_m.shape[1]
    do_conv = conv_w is not None
    do_route = router is not None
    if r == 1:
        mod_spec = pl.BlockSpec((1, 1, d), lambda b, i: (b, 0, 0))
    else:
        mod_spec = pl.BlockSpec((1, tm, d), lambda b, i: (b, i, 0))
    tok = lambda w: pl.BlockSpec((1, tm, w), lambda b, i: (b, i, 0))
    const2 = lambda b, i: (0, 0)
    args = [x, u]
    specs = [tok(d), tok(D_CONV)]
    if do_conv:
        per = tm // CONV_HALO
        args.append(u)
        specs.append(pl.BlockSpec((1, CONV_HALO, D_CONV), lambda b, i: (b, jnp.maximum(i * per - 1, 0), 0)))
    args += [attn, ga, gc, gate_m, shift_f, scale_f]
    specs += [pl.BlockSpec((1, N_HEADS, tm, HEAD_DIM), lambda b, i: (b, 0, i, 0)), tok(d), tok(d),
              mod_spec, mod_spec, mod_spec]
    if do_conv:
        args += [conv_w[0], conv_w[1]]
        specs += [pl.BlockSpec((CONV_W, D_CONV), const2), pl.BlockSpec((1, D_CONV), const2)]
    args += [ln_g, ln_b, w_pw_b, w_o_b, w_out_b, g2]
    specs += [pl.BlockSpec((1, D_CONV), const2), pl.BlockSpec((1, D_CONV), const2),
              pl.BlockSpec((D_CONV, d), const2),
              pl.BlockSpec((N_HEADS, HEAD_DIM, d), lambda b, i: (0, 0, 0)),
              pl.BlockSpec((d, d), const2), pl.BlockSpec((1, d), const2)]
    if do_route:
        args += [router[0], router[1]]
        specs += [pl.BlockSpec((d, N_EXPERTS), const2), pl.BlockSpec((1, N_EXPERTS), const2)]
    out_shape = [jax.ShapeDtypeStruct((g, tg, d), F32), jax.ShapeDtypeStruct((g, tg, d), BF16)]
    out_specs = [tok(d), tok(d)]
    if do_route:
        out_shape.append(jax.ShapeDtypeStruct((g, tg, N_EXPERTS), F32))
        out_specs.append(tok(N_EXPERTS))
    scratch = []
    if do_conv:
        scratch = [pltpu.VMEM((CONV_HALO + tm, D_CONV), F32), pltpu.VMEM((tm, D_CONV), F32)]
    outs = pl.pallas_call(
        functools.partial(_post_kernel, tm=tm, do_conv=do_conv, do_route=do_route),
        grid=(g, tg // tm),
        in_specs=specs,
        out_specs=tuple(out_specs),
        out_shape=tuple(out_shape),
        scratch_shapes=scratch,
        compiler_params=_cparams(("parallel", "arbitrary")),
        name="post_mixer",
    )(*args)
    return outs


def _ffn_kernel(h_ref, x_ref, gf_ref, w1_ref, w3_ref, w2_ref, o_ref, acc_ref):
    f = pl.program_id(2)

    @pl.when(f == 0)
    def _():
        acc_ref[...] = jnp.zeros_like(acc_ref)

    hb = h_ref[0]
    a1 = _dot(hb, w1_ref[...])
    a3 = _dot(hb, w3_ref[...])
    act = (a1 * _sigmoid(a1) * a3).astype(BF16)
    acc_ref[...] += _dot(act, w2_ref[...])

    @pl.when(f == pl.num_programs(2) - 1)
    def _():
        o_ref[0] = x_ref[0] + gf_ref[0] * acc_ref[...]


def _ffn_dense(h2, x, gate_f, w1b, w3b, w2b, *, tm, tf):
    g, tg, d = x.shape
    dff = w1b.shape[1]
    r = gate_f.shape[1]
    if r == 1:
        mod_spec = pl.BlockSpec((1, 1, d), lambda b, i, f: (b, 0, 0))
    else:
        mod_spec = pl.BlockSpec((1, tm, d), lambda b, i, f: (b, i, 0))
    tok = pl.BlockSpec((1, tm, d), lambda b, i, f: (b, i, 0))
    return pl.pallas_call(
        _ffn_kernel,
        grid=(g, tg // tm, dff // tf),
        in_specs=[tok, tok, mod_spec,
                  pl.BlockSpec((d, tf), lambda b, i, f: (0, f)),
                  pl.BlockSpec((d, tf), lambda b, i, f: (0, f)),
                  pl.BlockSpec((tf, d), lambda b, i, f: (f, 0))],
        out_specs=tok,
        out_shape=jax.ShapeDtypeStruct((g, tg, d), F32),
        scratch_shapes=[pltpu.VMEM((tm, d), F32)],
        compiler_params=_cparams(("parallel", "parallel", "arbitrary")),
        name="ffn_dense",
    )(h2, x, gate_f, w1b, w3b, w2b)


def _moe_kernel(h_ref, x_ref, gf_ref, gate_ref, w1_ref, w3_ref, w2_ref, o_ref, acc_ref):
    e = pl.program_id(2)
    f = pl.program_id(3)

    @pl.when((e == 0) & (f == 0))
    def _():
        acc_ref[...] = jnp.zeros_like(acc_ref)

    hb = h_ref[0]
    a1 = _dot(hb, w1_ref[...])
    a3 = _dot(hb, w3_ref[...])
    act = (a1 * _sigmoid(a1) * a3).astype(BF16)
    gates = gate_ref[0]
    eidx = lax.broadcasted_iota(jnp.int32, gates.shape, 1)
    ge = jnp.sum(jnp.where(eidx == e, gates, 0.0), axis=-1, keepdims=True)
    acc_ref[...] += ge * _dot(act, w2_ref[...])

    @pl.when((e == pl.num_programs(2) - 1) & (f == pl.num_programs(3) - 1))
    def _():
        o_ref[0] = x_ref[0] + gf_ref[0] * acc_ref[...]


def _ffn_moe(h2, x, gate_f, gates, w1b, w3b, w2b, *, tm, tf):
    g, tg, d = x.shape
    n_e, _, dff = w1b.shape
    r = gate_f.shape[1]
    if r == 1:
        mod_spec = pl.BlockSpec((1, 1, d), lambda b, i, e, f: (b, 0, 0))
    else:
        mod_spec = pl.BlockSpec((1, tm, d), lambda b, i, e, f: (b, i, 0))
    tok = pl.BlockSpec((1, tm, d), lambda b, i, e, f: (b, i, 0))
    return pl.pallas_call(
        _moe_kernel,
        grid=(g, tg // tm, n_e, dff // tf),
        in_specs=[tok, tok, mod_spec,
                  pl.BlockSpec((1, tm, n_e), lambda b, i, e, f: (b, i, 0)),
                  pl.BlockSpec((None, d, tf), lambda b, i, e, f: (e, 0, f)),
                  pl.BlockSpec((None, d, tf), lambda b, i, e, f: (e, 0, f)),
                  pl.BlockSpec((None, tf, d), lambda b, i, e, f: (e, f, 0))],
        out_specs=tok,
        out_shape=jax.ShapeDtypeStruct((g, tg, d), F32),
        scratch_shapes=[pltpu.VMEM((tm, d), F32)],
        compiler_params=_cparams(("parallel", "parallel", "arbitrary", "arbitrary")),
        name="ffn_moe",
    )(h2, x, gate_f, gates, w1b, w3b, w2b)


def kernel(x_prompt, x_sample, cache_k, cache_v, state_conv, page_table, c_prompt, c_sample, w_mod, b_mod,
           norm1_g, norm2_g, w_in, q_norm_g, k_norm_g, w_o_attn, w_dw, b_dw, conv_ln_g, conv_ln_b, w_pw_conv,
           w_out, ffn_w1, ffn_w3, ffn_w2, router_w, router_b, moe_w1, moe_w3, moe_w2):
    depth = w_in.shape[0]
    n_p, t_p, d = x_prompt.shape
    n_s, t_s, _ = x_sample.shape
    n_tok_s = n_s * t_s
    n_phys = cache_k.shape[1]

    c_all = jnp.concatenate([c_prompt, c_sample], axis=0)
    pad = (-c_all.shape[0]) % 8
    c_all = jnp.pad(c_all, ((0, pad), (0, 0)))
    mod = _modulation(c_all, w_mod, b_mod)

    cache_k4 = cache_k.reshape(depth, n_phys, PAGE_SIZE * N_HEADS, HEAD_DIM)
    cache_v4 = cache_v.reshape(depth, n_phys, PAGE_SIZE * N_HEADS, HEAD_DIM)

    head_id = jnp.arange(D_ATTN) // HEAD_DIM
    head_mean = jnp.where(head_id[:, None] == head_id[None, :], 1.0 / HEAD_DIM, 0.0).astype(BF16)

    xp = x_prompt
    xs = x_sample.reshape(1, n_tok_s, d)
    k_p, v_p, conv_p, k_s, v_s, conv_s = [], [], [], [], [], []
    for l in range(depth):
        mod_p = mod[l, :n_p].reshape(n_p, N_MOD, 1, d)
        mod_s = jnp.repeat(mod[l, n_p:n_p + n_s].reshape(n_s, N_MOD, d), t_s, axis=0)
        mp = [mod_p[:, i] for i in range(N_MOD)]
        msm = [mod_s[None, :, i] for i in range(N_MOD)]

        w_in_b = w_in[l].astype(BF16)
        g1 = norm1_g[l].reshape(1, d)
        g2 = norm2_g[l].reshape(1, d)
        gq = jnp.tile(q_norm_g[l], N_HEADS).reshape(1, D_ATTN)
        gk = jnp.tile(k_norm_g[l], N_HEADS).reshape(1, D_ATTN)
        w_o_b = w_o_attn[l].astype(BF16).reshape(N_HEADS, HEAD_DIM, d)
        w_pw_b = w_pw_conv[l].astype(BF16)
        w_out_b = w_out[l].astype(BF16)
        ln_g = conv_ln_g[l].reshape(1, D_CONV)
        ln_b = conv_ln_b[l].reshape(1, D_CONV)
        is_moe = l % 2 == 1
        router = (router_w[l // 2], router_b[l // 2].reshape(1, N_EXPERTS)) if is_moe else None

        q, kh, vh, kf, vf, u, ga, gc, km = _in_projection(xp, mp[0], mp[1], g1, w_in_b, head_mean, gq, gk, tm=256)
        kmean = km.reshape(n_p, t_p // MOBA_BLOCK, N_HEADS, HEAD_DIM).transpose(0, 2, 1, 3)
        attn = _moba_prompt(q, kh, vh, kmean)
        outs = _post_mixer(xp, u, attn, ga, gc, mp[2], mp[3], mp[4], (w_dw[l], b_dw[l].reshape(1, D_CONV)),
                           ln_g, ln_b, w_pw_b, w_o_b, w_out_b, g2, router, tm=256)
        k_p.append(kf.reshape(n_p, t_p, N_HEADS, HEAD_DIM))
        v_p.append(vf.reshape(n_p, t_p, N_HEADS, HEAD_DIM))
        conv_p.append(u[:, t_p - (CONV_W - 1):])
        xp_mid, h2_p = outs[0], outs[1]
        gates_p = outs[2] if is_moe else None

        qs, _, _, kfs, vfs, us, gas, gcs, _ = _in_projection(xs, msm[0], msm[1], g1, w_in_b, head_mean, gq, gk,
                                                             tm=256)
        q_all = qs.reshape(N_HEADS, n_s, t_s, HEAD_DIM).transpose(1, 0, 2, 3).reshape(n_s, N_HEADS * t_s, HEAD_DIM)
        k_new = kfs.reshape(n_s, t_s * N_HEADS, HEAD_DIM)
        v_new = vfs.reshape(n_s, t_s * N_HEADS, HEAD_DIM)
        o_t = _moba_sample(page_table, q_all, k_new, v_new, cache_k4, cache_v4, l)
        attn_s = o_t.reshape(n_s, HEAD_DIM, N_HEADS, t_s).transpose(2, 0, 3, 1)
        attn_s = attn_s.reshape(1, N_HEADS, n_tok_s, HEAD_DIM).astype(BF16)
        u_s = us.reshape(n_s, t_s, D_CONV)
        y_dw = _conv_sample(state_conv[l], u_s, w_dw[l], b_dw[l]).reshape(1, n_tok_s, D_CONV)
        outs = _post_mixer(xs, y_dw, attn_s, gas, gcs, msm[2], msm[3], msm[4], None,
                           ln_g, ln_b, w_pw_b, w_o_b, w_out_b, g2, router, tm=256)
        k_s.append(kfs.reshape(n_s, t_s, N_HEADS, HEAD_DIM))
        v_s.append(vfs.reshape(n_s, t_s, N_HEADS, HEAD_DIM))
        conv_s.append(jnp.concatenate([state_conv[l], u_s], axis=1)[:, -(CONV_W - 1):])
        xs_mid, h2_s = outs[0], outs[1]
        gates_s = outs[2] if is_moe else None

        if is_moe:
            w1b = moe_w1[l // 2].astype(BF16)
            w3b = moe_w3[l // 2].astype(BF16)
            w2b = moe_w2[l // 2].astype(BF16)
            xp = _ffn_moe(h2_p, xp_mid, mp[5], gates_p, w1b, w3b, w2b, tm=512, tf=1408)
            xs = _ffn_moe(h2_s, xs_mid, msm[5], gates_s, w1b, w3b, w2b, tm=512, tf=1408)
        else:
            w1b = ffn_w1[l // 2].astype(BF16)
            w3b = ffn_w3[l // 2].astype(BF16)
            w2b = ffn_w2[l // 2].astype(BF16)
            xp = _ffn_dense(h2_p, xp_mid, mp[5], w1b, w3b, w2b, tm=512, tf=1408)
            xs = _ffn_dense(h2_s, xs_mid, msm[5], w1b, w3b, w2b, tm=512, tf=1408)

    return (xp, xs.reshape(n_s, t_s, d), jnp.stack(k_p), jnp.stack(v_p), jnp.stack(conv_p),
            jnp.stack(k_s), jnp.stack(v_s), jnp.stack(conv_s))
```

```python
import functools

import jax
import jax.numpy as jnp
from jax import lax
from jax.experimental import pallas as pl
from jax.experimental.pallas import tpu as pltpu

F32 = jnp.float32
BF16 = jnp.bfloat16
HIGHEST = lax.Precision.HIGHEST

D_MODEL = 1024
N_HEADS = 8
HEAD_DIM = 64
D_ATTN = N_HEADS * HEAD_DIM
MOBA_BLOCK = 256
MOBA_TOPK = 3
D_CONV = 512
CONV_W = 31
N_EXPERTS = 8
N_MOD = 6
PAGE_SIZE = 128
NORM_EPS = 1e-6
LN_EPS = 1e-5
NEG = -1e30
LANES = 128
CONV_HALO = 32
VMEM_LIMIT = 56 * 1024 * 1024


def _cparams(sem):
    return pltpu.CompilerParams(dimension_semantics=sem, vmem_limit_bytes=VMEM_LIMIT)


def _dot(a, b):
    return jnp.dot(a, b, preferred_element_type=F32)


def _dot_nt(a, b, precision=None):
    return lax.dot_general(a, b, (((1,), (1,)), ((), ())), precision=precision, preferred_element_type=F32)


def _sigmoid(x):
    return 1.0 / (1.0 + jnp.exp(-x))


def _split_bf16(x):
    hi = x.astype(BF16)
    lo = (x - hi.astype(F32)).astype(BF16)
    return hi, lo


def _top_select(score, valid, idx, n_sel, axis, sentinel):
    remaining = valid
    sel = jnp.zeros(score.shape, dtype=jnp.bool_)
    for _ in range(n_sel):
        cur = jnp.where(remaining, score, -jnp.inf)
        best = jnp.max(cur, axis=axis, keepdims=True)
        cand = remaining & (cur == best)
        first = jnp.min(jnp.where(cand, idx, sentinel), axis=axis, keepdims=True)
        pick = idx == first
        sel = sel | pick
        remaining = remaining & jnp.logical_not(pick)
    return sel


def _mod_kernel(c_ref, w_ref, b_ref, o_ref):
    c = c_ref[...]
    cond = c * _sigmoid(c)
    o_ref[0] = _dot(cond.astype(BF16), w_ref[0].astype(BF16)) + b_ref[0]


def _modulation(c_all, w_mod, b_mod):
    depth, d, n = w_mod.shape
    rows = c_all.shape[0]
    tn = 1536
    return pl.pallas_call(
        _mod_kernel,
        grid=(depth, n // tn),
        in_specs=[
            pl.BlockSpec((rows, d), lambda l, j: (0, 0)),
            pl.BlockSpec((1, d, tn), lambda l, j: (l, 0, j)),
            pl.BlockSpec((1, 1, tn), lambda l, j: (l, 0, j)),
        ],
        out_specs=pl.BlockSpec((1, rows, tn), lambda l, j: (l, 0, j)),
        out_shape=jax.ShapeDtypeStruct((depth, rows, n), F32),
        compiler_params=_cparams(("parallel", "parallel")),
        name="adaln_modulation",
    )(c_all, w_mod, b_mod.reshape(depth, 1, n))


def _inproj_kernel(x_ref, shift_ref, scale_ref, g1_ref, w_ref, hm_ref, gq_ref, gk_ref, *out_refs, tm, head_major):
    if head_major:
        qp_ref, ka_ref, va_ref, kf_ref, vf_ref, u_ref, ga_ref, gc_ref, km_ref = out_refs
    else:
        qf_ref, kf_ref, vf_ref, u_ref, ga_ref, gc_ref = out_refs
    x = x_ref[0]
    ms = jnp.mean(x * x, axis=-1, keepdims=True)
    h = (x * lax.rsqrt(ms + NORM_EPS)) * g1_ref[...]
    h = h * (1.0 + scale_ref[0]) + shift_ref[0]
    hb = h.astype(BF16)

    def proj(lo, hi):
        return _dot(hb, w_ref[:, lo:hi])

    def head_rms(z, g_ref):
        z2_hi, z2_lo = _split_bf16(z * z)
        msh = _dot(z2_hi, hm_ref[...]) + _dot(z2_lo, hm_ref[...])
        return z * lax.rsqrt(msh + NORM_EPS) * g_ref[...]

    b0, b1, b2, b3, b4, b5, b6 = (0, D_ATTN, 2 * D_ATTN, 3 * D_ATTN, 3 * D_ATTN + D_CONV,
                                  3 * D_ATTN + 2 * D_CONV, 3 * D_ATTN + 2 * D_CONV + D_MODEL)
    q = head_rms(proj(b0, b1), gq_ref)
    k = head_rms(proj(b1, b2), gk_ref)
    v = proj(b2, b3)
    kf_ref[0] = k
    vf_ref[0] = v
    if head_major:
        blk = pl.program_id(1) * (tm // MOBA_BLOCK)
        lane = lax.broadcasted_iota(jnp.int32, (tm, LANES), 1)
        low = lane < HEAD_DIM
        k_tail = jnp.where(lane == HEAD_DIM + blk, 1.0, 0.0)
        v_tail = jnp.where(lane == HEAD_DIM, 1.0, 0.0)
        for hp in range(N_HEADS // 2):
            sl = slice(hp * LANES, (hp + 1) * LANES)
            for z, dst, tail in ((q, qp_ref, 0.0), (k, ka_ref, k_tail), (v, va_ref, v_tail)):
                pair = z[:, sl]
                dst[0, 2 * hp] = jnp.where(low, pair, tail).astype(dst.dtype)
                dst[0, 2 * hp + 1] = jnp.where(low, pltpu.roll(pair, HEAD_DIM, 1), tail).astype(dst.dtype)
        km_ref[0, 0] = jnp.mean(k, axis=0, keepdims=True)
    else:
        qf_ref[0] = q
    u_ref[0] = proj(b3, b4) * _sigmoid(proj(b4, b5))
    ga_ref[0] = _sigmoid(proj(b5, b6)).astype(BF16)
    gc_ref[0] = _sigmoid(proj(b6, b6 + D_MODEL)).astype(BF16)


def _in_projection(x, shift, scale, g1, w_in_b, head_mean, gq, gk, *, tm, head_major):
    g, tg, d = x.shape
    r = shift.shape[1]
    n_in = w_in_b.shape[1]
    if r == 1:
        mod_spec = pl.BlockSpec((1, 1, d), lambda b, i: (b, 0, 0))
    else:
        mod_spec = pl.BlockSpec((1, tm, d), lambda b, i: (b, i, 0))
    const2 = lambda b, i: (0, 0)
    tok = lambda w: pl.BlockSpec((1, tm, w), lambda b, i: (b, i, 0))
    tok_shape = lambda w, dt: jax.ShapeDtypeStruct((g, tg, w), dt)
    common_shapes = (tok_shape(D_ATTN, F32), tok_shape(D_ATTN, F32), tok_shape(D_CONV, F32),
                     tok_shape(d, BF16), tok_shape(d, BF16))
    common_specs = (tok(D_ATTN), tok(D_ATTN), tok(D_CONV), tok(d), tok(d))
    if head_major:
        assert tm == MOBA_BLOCK
        heads = pl.BlockSpec((1, N_HEADS, tm, LANES), lambda b, i: (b, 0, i, 0))
        hshape = lambda dt: jax.ShapeDtypeStruct((g, N_HEADS, tg, LANES), dt)
        out_shapes = (hshape(F32), hshape(BF16), hshape(BF16)) + common_shapes + (
            jax.ShapeDtypeStruct((g, tg // MOBA_BLOCK, 1, D_ATTN), F32),)
        out_specs = (heads, heads, heads) + common_specs + (
            pl.BlockSpec((1, 1, 1, D_ATTN), lambda b, i: (b, i, 0, 0)),)
    else:
        out_shapes = (tok_shape(D_ATTN, F32),) + common_shapes
        out_specs = (tok(D_ATTN),) + common_specs
    return pl.pallas_call(
        functools.partial(_inproj_kernel, tm=tm, head_major=head_major),
        grid=(g, tg // tm),
        in_specs=[tok(d), mod_spec, mod_spec,
                  pl.BlockSpec((1, d), const2),
                  pl.BlockSpec((d, n_in), const2),
                  pl.BlockSpec((D_ATTN, D_ATTN), const2),
                  pl.BlockSpec((1, D_ATTN), const2),
                  pl.BlockSpec((1, D_ATTN), const2)],
        out_specs=out_specs,
        out_shape=out_shapes,
        compiler_params=_cparams(("parallel", "parallel")),
        name="in_projection",
    )(x, shift, scale, g1, w_in_b, head_mean, gq, gk)


def _moba_prompt_kernel(q_ref, k_ref, v_ref, km_ref, o_ref, qa_scr, s_scr, m_scr, acc_scr, *, n_blk, hps):
    j = pl.program_id(2)
    tq = MOBA_BLOCK
    tk = 2 * MOBA_BLOCK
    own_pair = j // 2
    lane = lax.broadcasted_iota(jnp.int32, (tq, LANES), 1)
    blk_t = lax.broadcasted_iota(jnp.int32, (n_blk, tq), 0)

    def lane_max(s):
        m = s[:, :LANES]
        for c in range(1, tk // LANES):
            m = jnp.maximum(m, s[:, c * LANES:(c + 1) * LANES])
        return m

    for hh in range(hps):
        qp = q_ref[0, hh]
        gate_t = _dot_nt(km_ref[0, hh], qp, precision=HIGHEST)
        sel_t = _top_select(gate_t, blk_t < j, blk_t, min(MOBA_TOPK, n_blk), 0, n_blk)
        bias_t = jnp.where(sel_t | (blk_t == j), 0.0, NEG)
        bias_rows = jnp.concatenate([jnp.zeros((HEAD_DIM, tq), F32), bias_t,
                                     jnp.zeros((LANES - HEAD_DIM - n_blk, tq), F32)], axis=0)
        qa_scr[hh] = (qp * (HEAD_DIM ** -0.5) + jnp.transpose(bias_rows)).astype(BF16)

    own_off = (j & 1) * tq
    row = lax.broadcasted_iota(jnp.int32, (tq, tk), 0)
    col = lax.broadcasted_iota(jnp.int32, (tq, tk), 1)
    hidden = col - own_off > row
    st_own = pl.multiple_of(own_pair * tk, tk)
    for hh in range(hps):
        s = _dot_nt(qa_scr[hh], k_ref[0, hh, pl.ds(st_own, tk), :])
        s = jnp.where(hidden, NEG, s)
        s_scr[hh, own_pair] = s
        m_scr[hh] = lane_max(s)

    def pass1(ip, carry):
        st = pl.multiple_of(ip * tk, tk)
        for hh in range(hps):
            s = _dot_nt(qa_scr[hh], k_ref[0, hh, pl.ds(st, tk), :])
            s_scr[hh, ip] = s
            m_scr[hh] = jnp.maximum(m_scr[hh], lane_max(s))
        return carry

    lax.fori_loop(0, own_pair, pass1, 0)
    for hh in range(hps):
        m = jnp.max(m_scr[hh], axis=-1, keepdims=True)
        m_scr[hh] = jnp.broadcast_to(m, (tq, LANES))
        acc_scr[hh] = jnp.zeros((tq, LANES), F32)

    def pass2(ip, carry):
        st = pl.multiple_of(ip * tk, tk)
        for hh in range(hps):
            m = m_scr[hh]
            p = jnp.concatenate(
                [jnp.exp(s_scr[hh, ip, :, c * LANES:(c + 1) * LANES] - m).astype(BF16) for c in range(tk // LANES)],
                axis=-1)
            acc_scr[hh] += _dot(p, v_ref[0, hh, pl.ds(st, tk), :])
        return carry

    lax.fori_loop(0, own_pair + 1, pass2, 0)
    outs = []
    for hh in range(hps):
        acc = acc_scr[hh]
        denom = jnp.sum(jnp.where(lane == HEAD_DIM, acc, 0.0), axis=-1, keepdims=True)
        outs.append(acc / denom)
    slabs = [jnp.where(lane < HEAD_DIM, outs[2 * i], pltpu.roll(outs[2 * i + 1], HEAD_DIM, 1))
             for i in range(hps // 2)]
    o_ref[0] = jnp.concatenate(slabs, axis=-1).astype(o_ref.dtype)


def _moba_prompt(q_pad, k_aug, v_aug, kmean_pad, *, hps=4):
    b, h, t, _ = q_pad.shape
    n_blk = t // MOBA_BLOCK
    assert HEAD_DIM + n_blk <= LANES and n_blk % 2 == 0 and h % hps == 0 and hps % 2 == 0
    tq = MOBA_BLOCK
    return pl.pallas_call(
        functools.partial(_moba_prompt_kernel, n_blk=n_blk, hps=hps),
        grid=(b, h // hps, n_blk),
        in_specs=[pl.BlockSpec((1, hps, tq, LANES), lambda bi, hg, j: (bi, hg, j, 0)),
                  pl.BlockSpec((1, hps, t, LANES), lambda bi, hg, j: (bi, hg, 0, 0)),
                  pl.BlockSpec((1, hps, t, LANES), lambda bi, hg, j: (bi, hg, 0, 0)),
                  pl.BlockSpec((1, hps, n_blk, LANES), lambda bi, hg, j: (bi, hg, 0, 0))],
        out_specs=pl.BlockSpec((1, tq, hps * HEAD_DIM), lambda bi, hg, j: (bi, j, hg)),
        out_shape=jax.ShapeDtypeStruct((b, t, h * HEAD_DIM), BF16),
        scratch_shapes=[pltpu.VMEM((hps, tq, LANES), BF16),
                        pltpu.VMEM((hps, n_blk // 2, tq, 2 * tq), F32),
                        pltpu.VMEM((hps, tq, LANES), F32),
                        pltpu.VMEM((hps, tq, LANES), F32)],
        compiler_params=_cparams(("parallel", "parallel", "arbitrary")),
        name="moba_prompt_attention",
    )(q_pad, k_aug, v_aug, kmean_pad)


def _moba_sample_kernel(pt_ref, q_ref, knew_ref, vnew_ref, *refs, n_pages, n_q):
    del pt_ref
    k_refs = refs[:n_pages]
    v_refs = refs[n_pages:2 * n_pages]
    o_ref = refs[2 * n_pages]
    s_scr = refs[2 * n_pages + 1]
    n_row = n_q * N_HEADS
    pages_per_blk = MOBA_BLOCK // PAGE_SIZE
    n_blk = n_pages // pages_per_blk
    h_shift = N_HEADS.bit_length() - 1
    d_shift = HEAD_DIM.bit_length() - 1
    scale = HEAD_DIM ** -0.5

    q = q_ref[0]
    q_rep = jnp.concatenate([jnp.broadcast_to(q[i:i + 1, :], (N_HEADS, D_ATTN)) for i in range(n_q)], axis=0)
    r_i = lax.broadcasted_iota(jnp.int32, (n_row, D_ATTN), 0)
    c_i = lax.broadcasted_iota(jnp.int32, (n_row, D_ATTN), 1)
    own_head = (r_i & (N_HEADS - 1)) == (c_i >> d_shift)
    q_bd = jnp.where(own_head, q_rep, 0.0)
    q_hi, q_lo = _split_bf16(q_bd)

    lane = lax.broadcasted_iota(jnp.int32, (n_row, PAGE_SIZE), 1)
    gate = jnp.zeros((n_row, PAGE_SIZE), F32)
    for pg in range(n_pages):
        k_hi, k_lo = _split_bf16(k_refs[pg][...])
        s = _dot(q_hi, k_hi) + (_dot(q_lo, k_hi) + _dot(q_hi, k_lo))
        s_scr[pg] = s
        tot = jnp.sum(s, axis=-1, keepdims=True)
        gate = gate + jnp.where(lane == pg // pages_per_blk, tot, 0.0)
    sel = _top_select(gate * (1.0 / MOBA_BLOCK), lane < n_blk, lane, min(MOBA_TOPK, n_blk + 1), 1, PAGE_SIZE)
    sel_f = jnp.where(sel, 1.0, 0.0)

    knew = knew_ref[0]
    vnew = vnew_ref[0]
    q_idx = lax.broadcasted_iota(jnp.int32, (n_row, 1), 0) >> h_shift
    s_new = [jnp.where(q_idx >= jj, jnp.sum(q_bd * knew[jj:jj + 1, :], axis=-1, keepdims=True) * scale, NEG)
             for jj in range(n_q)]
    m = s_new[0]
    for jj in range(1, n_q):
        m = jnp.maximum(m, s_new[jj])
    chosen = [jnp.max(jnp.where(lane == bi, sel_f, 0.0), axis=-1, keepdims=True) > 0.0 for bi in range(n_blk)]
    mpart = jnp.full((n_row, PAGE_SIZE), NEG, F32)
    for pg in range(n_pages):
        s = jnp.where(chosen[pg // pages_per_blk], s_scr[pg] * scale, NEG)
        s_scr[pg] = s
        mpart = jnp.maximum(mpart, s)
    m = jnp.maximum(m, jnp.max(mpart, axis=-1, keepdims=True))

    acc = jnp.zeros((n_row, D_ATTN), F32)
    l = jnp.zeros((n_row, 1), F32)
    for jj in range(n_q):
        p = jnp.exp(s_new[jj] - m)
        l = l + p
        acc = acc + p * vnew[jj:jj + 1, :]
    lpart = jnp.zeros((n_row, PAGE_SIZE), F32)
    for pg in range(n_pages):
        p = jnp.exp(s_scr[pg] - m)
        lpart = lpart + p
        acc = acc + _dot_nt(p.astype(BF16), v_refs[pg][...].astype(BF16))
    l = l + jnp.sum(lpart, axis=-1, keepdims=True)
    out = jnp.where(own_head, acc / l, 0.0).reshape(n_q, N_HEADS, D_ATTN)
    o_ref[0] = jnp.sum(out, axis=1)


def _moba_sample(page_table, q, k_new, v_new, cache_kt, cache_vt, layer):
    n_seq, n_pages = page_table.shape
    n_q = q.shape[1]
    seq3 = pl.BlockSpec((1, n_q, D_ATTN), lambda b, pt: (b, 0, 0))

    def page_spec(pg):
        return pl.BlockSpec((None, None, D_ATTN, PAGE_SIZE), lambda b, pt, pg=pg: (layer, pt[b, pg], 0, 0))

    in_specs = [seq3, seq3, seq3] + [page_spec(pg) for pg in range(n_pages)] * 2
    grid_spec = pltpu.PrefetchScalarGridSpec(
        num_scalar_prefetch=1, grid=(n_seq,), in_specs=in_specs, out_specs=seq3,
        scratch_shapes=[pltpu.VMEM((n_pages, n_q * N_HEADS, PAGE_SIZE), F32)])
    return pl.pallas_call(
        functools.partial(_moba_sample_kernel, n_pages=n_pages, n_q=n_q),
        grid_spec=grid_spec,
        out_shape=jax.ShapeDtypeStruct((n_seq, n_q, D_ATTN), F32),
        compiler_params=_cparams(("arbitrary",)),
        name="moba_sample_attention",
    )(page_table, q, k_new, v_new, *([cache_kt] * n_pages), *([cache_vt] * n_pages))


def _conv_sample_kernel(st_ref, u_ref, wst_ref, wu_ref, b_ref, o_ref, *, n_q):
    st = st_ref[...]
    u = u_ref[...]
    for t in range(n_q):
        y = jnp.sum(st * wst_ref[t], axis=1) + jnp.sum(u * wu_ref[t], axis=1)
        o_ref[t] = y + b_ref[...]


def _conv_sample(state, u, w_dw, b_dw):
    n_seq, ctx, c = state.shape
    n_q = u.shape[1]
    r = jnp.arange(ctx)[None, :] - jnp.arange(n_q)[:, None]
    w_state = jnp.where((r >= 0)[..., None], w_dw[jnp.clip(r, 0, CONV_W - 1)], 0.0)
    ju = ctx + jnp.arange(n_q)[None, :] - jnp.arange(n_q)[:, None]
    w_new = jnp.where((ju <= ctx)[..., None], w_dw[jnp.clip(ju, 0, CONV_W - 1)], 0.0)
    sb = 16
    out = pl.pallas_call(
        functools.partial(_conv_sample_kernel, n_q=n_q),
        grid=(n_seq // sb,),
        in_specs=[pl.BlockSpec((sb, ctx, c), lambda i: (i, 0, 0)),
                  pl.BlockSpec((sb, n_q, c), lambda i: (i, 0, 0)),
                  pl.BlockSpec((n_q, ctx, c), lambda i: (0, 0, 0)),
                  pl.BlockSpec((n_q, n_q, c), lambda i: (0, 0, 0)),
                  pl.BlockSpec((1, c), lambda i: (0, 0))],
        out_specs=pl.BlockSpec((n_q, sb, c), lambda i: (0, i, 0)),
        out_shape=jax.ShapeDtypeStruct((n_q, n_seq, c), F32),
        compiler_params=_cparams(("parallel",)),
        name="conv_sample",
    )(state, u, w_state, w_new, b_dw.reshape(1, c))
    return out.transpose(1, 0, 2)


def _post_kernel(*refs, tm, do_conv, do_route):
    it = iter(refs)
    x_ref, u_ref = next(it), next(it)
    halo_ref = next(it) if do_conv else None
    attn_ref, ga_ref, gc_ref, gm_ref, sf_ref, scf_ref = (next(it) for _ in range(6))
    if do_conv:
        wdw_ref, bdw_ref = next(it), next(it)
    lng_ref, lnb_ref, wpw_ref, wo_ref, wout_ref, g2_ref = (next(it) for _ in range(6))
    if do_route:
        rwh_ref, rwl_ref, rb_ref = next(it), next(it), next(it)
    xo_ref, h2_ref = next(it), next(it)
    gate_ref = next(it) if do_route else None
    if do_conv:
        xbuf, ybuf = next(it), next(it)

    if do_conv:
        i = pl.program_id(1)
        xbuf[0:CONV_HALO, :] = jnp.where(i > 0, halo_ref[0], 0.0)
        xbuf[CONV_HALO:CONV_HALO + tm, :] = u_ref[0]
        off = CONV_HALO - (CONV_W - 1)
        for c0 in range(0, D_CONV, LANES):
            acc = jnp.broadcast_to(bdw_ref[:, c0:c0 + LANES], (tm, LANES))
            for w in range(CONV_W):
                acc = acc + xbuf[off + w:off + w + tm, c0:c0 + LANES] * wdw_ref[w:w + 1, c0:c0 + LANES]
            ybuf[:, c0:c0 + LANES] = acc
        yc = ybuf[...]
    else:
        yc = u_ref[0]

    mu = jnp.mean(yc, axis=-1, keepdims=True)
    var = jnp.mean(jnp.square(yc - mu), axis=-1, keepdims=True)
    yn = (yc - mu) * lax.rsqrt(var + LN_EPS) * lng_ref[...] + lnb_ref[...]
    y_conv = _dot((yn * _sigmoid(yn)).astype(BF16), wpw_ref[...])
    y_attn = _dot(attn_ref[0], wo_ref[...])
    merged = ga_ref[0].astype(F32) * y_attn + gc_ref[0].astype(F32) * y_conv
    xn = x_ref[0] + gm_ref[0] * _dot(merged.astype(BF16), wout_ref[...])
    xo_ref[0] = xn
    ms = jnp.mean(xn * xn, axis=-1, keepdims=True)
    h2 = (xn * lax.rsqrt(ms + NORM_EPS)) * g2_ref[...]
    h2 = h2 * (1.0 + scf_ref[0]) + sf_ref[0]
    h2_ref[0] = h2.astype(BF16)
    if do_route:
        h_hi, h_lo = _split_bf16(h2)
        logits = (_dot(h_hi, rwh_ref[...]) + (_dot(h_lo, rwh_ref[...]) + _dot(h_hi, rwl_ref[...]))) + rb_ref[...]
        eidx = lax.broadcasted_iota(jnp.int32, logits.shape, 1)
        top1 = jnp.max(logits, axis=-1, keepdims=True)
        i1 = jnp.min(jnp.where(logits == top1, eidx, N_EXPERTS), axis=-1, keepdims=True)
        rest = jnp.where(eidx == i1, -jnp.inf, logits)
        top2 = jnp.max(rest, axis=-1, keepdims=True)
        i2 = jnp.min(jnp.where(rest == top2, eidx, N_EXPERTS), axis=-1, keepdims=True)
        e2 = jnp.exp(top2 - top1)
        w1 = 1.0 / (1.0 + e2)
        w2 = e2 / (1.0 + e2)
        gate_ref[0] = jnp.where(eidx == i1, w1, 0.0) + jnp.where(eidx == i2, w2, 0.0)


def _post_mixer(x, u, attn, ga, gc, gate_m, shift_f, scale_f, conv_w, ln_g, ln_b, w_pw_b, w_o_b, w_out_b, g2,
                router, *, tm):
    g, tg, d = x.shape
    r = gate_m.shape[1]
    do_conv = conv_w is not None
    do_route = router is not None
    if r == 1:
        mod_spec = pl.BlockSpec((1, 1, d), lambda b, i: (b, 0, 0))
    else:
        mod_spec = pl.BlockSpec((1, tm, d), lambda b, i: (b, i, 0))
    tok = lambda w: pl.BlockSpec((1, tm, w), lambda b, i: (b, i, 0))
    const2 = lambda b, i: (0, 0)
    args = [x, u]
    specs = [tok(d), tok(D_CONV)]
    if do_conv:
        per = tm // CONV_HALO
        args.append(u)
        specs.append(pl.BlockSpec((1, CONV_HALO, D_CONV), lambda b, i: (b, jnp.maximum(i * per - 1, 0), 0)))
    args += [attn, ga, gc, gate_m, shift_f, scale_f]
    specs += [tok(D_ATTN), tok(d), tok(d), mod_spec, mod_spec, mod_spec]
    if do_conv:
        args += [conv_w[0], conv_w[1]]
        specs += [pl.BlockSpec((CONV_W, D_CONV), const2), pl.BlockSpec((1, D_CONV), const2)]
    args += [ln_g, ln_b, w_pw_b, w_o_b, w_out_b, g2]
    specs += [pl.BlockSpec((1, D_CONV), const2), pl.BlockSpec((1, D_CONV), const2),
              pl.BlockSpec((D_CONV, d), const2), pl.BlockSpec((D_ATTN, d), const2),
              pl.BlockSpec((d, d), const2), pl.BlockSpec((1, d), const2)]
    if do_route:
        args += list(router)
        specs += [pl.BlockSpec((d, N_EXPERTS), const2), pl.BlockSpec((d, N_EXPERTS), const2),
                  pl.BlockSpec((1, N_EXPERTS), const2)]
    out_shape = [jax.ShapeDtypeStruct((g, tg, d), F32), jax.ShapeDtypeStruct((g, tg, d), BF16)]
    out_specs = [tok(d), tok(d)]
    if do_route:
        out_shape.append(jax.ShapeDtypeStruct((g, tg, N_EXPERTS), F32))
        out_specs.append(tok(N_EXPERTS))
    scratch = []
    if do_conv:
        scratch = [pltpu.VMEM((CONV_HALO + tm, D_CONV), F32), pltpu.VMEM((tm, D_CONV), F32)]
    outs = pl.pallas_call(
        functools.partial(_post_kernel, tm=tm, do_conv=do_conv, do_route=do_route),
        grid=(g, tg // tm),
        in_specs=specs,
        out_specs=tuple(out_specs),
        out_shape=tuple(out_shape),
        scratch_shapes=scratch,
        compiler_params=_cparams(("parallel", "arbitrary")),
        name="post_mixer",
    )(*args)
    return outs


def _ffn_kernel(h_ref, x_ref, gf_ref, w1_ref, w3_ref, w2_ref, o_ref, acc_ref):
    f = pl.program_id(2)

    @pl.when(f == 0)
    def _():
        acc_ref[...] = jnp.zeros_like(acc_ref)

    hb = h_ref[0]
    a1 = _dot(hb, w1_ref[...])
    a3 = _dot(hb, w3_ref[...])
    act = (a1 * _sigmoid(a1) * a3).astype(BF16)
    acc_ref[...] += _dot(act, w2_ref[...])

    @pl.when(f == pl.num_programs(2) - 1)
    def _():
        o_ref[0] = x_ref[0] + gf_ref[0] * acc_ref[...]


def _ffn_dense(h2, x, gate_f, w1b, w3b, w2b, *, tm, tf):
    g, tg, d = x.shape
    dff = w1b.shape[1]
    tm = min(tm, tg)
    r = gate_f.shape[1]
    if r == 1:
        mod_spec = pl.BlockSpec((1, 1, d), lambda b, i, f: (b, 0, 0))
    else:
        mod_spec = pl.BlockSpec((1, tm, d), lambda b, i, f: (b, i, 0))
    tok = pl.BlockSpec((1, tm, d), lambda b, i, f: (b, i, 0))
    return pl.pallas_call(
        _ffn_kernel,
        grid=(g, tg // tm, dff // tf),
        in_specs=[tok, tok, mod_spec,
                  pl.BlockSpec((d, tf), lambda b, i, f: (0, f)),
                  pl.BlockSpec((d, tf), lambda b, i, f: (0, f)),
                  pl.BlockSpec((tf, d), lambda b, i, f: (f, 0))],
        out_specs=tok,
        out_shape=jax.ShapeDtypeStruct((g, tg, d), F32),
        scratch_shapes=[pltpu.VMEM((tm, d), F32)],
        compiler_params=_cparams(("parallel", "parallel", "arbitrary")),
        name="ffn_dense",
    )(h2, x, gate_f, w1b, w3b, w2b)


def _moe_kernel(h_ref, x_ref, gf_ref, gate_ref, w1_ref, w3_ref, w2_ref, o_ref, acc_ref):
    e = pl.program_id(2)
    f = pl.program_id(3)

    @pl.when((e == 0) & (f == 0))
    def _():
        acc_ref[...] = jnp.zeros_like(acc_ref)

    hb = h_ref[0]
    a1 = _dot(hb, w1_ref[...])
    a3 = _dot(hb, w3_ref[...])
    act = (a1 * _sigmoid(a1) * a3).astype(BF16)
    gates = gate_ref[0]
    eidx = lax.broadcasted_iota(jnp.int32, gates.shape, 1)
    ge = jnp.sum(jnp.where(eidx == e, gates, 0.0), axis=-1, keepdims=True)
    acc_ref[...] += ge * _dot(act, w2_ref[...])

    @pl.when((e == pl.num_programs(2) - 1) & (f == pl.num_programs(3) - 1))
    def _():
        o_ref[0] = x_ref[0] + gf_ref[0] * acc_ref[...]


def _ffn_moe(h2, x, gate_f, gates, w1b, w3b, w2b, *, tm, tf):
    g, tg, d = x.shape
    n_e, _, dff = w1b.shape
    tm = min(tm, tg)
    r = gate_f.shape[1]
    if r == 1:
        mod_spec = pl.BlockSpec((1, 1, d), lambda b, i, e, f: (b, 0, 0))
    else:
        mod_spec = pl.BlockSpec((1, tm, d), lambda b, i, e, f: (b, i, 0))
    tok = pl.BlockSpec((1, tm, d), lambda b, i, e, f: (b, i, 0))
    return pl.pallas_call(
        _moe_kernel,
        grid=(g, tg // tm, n_e, dff // tf),
        in_specs=[tok, tok, mod_spec,
                  pl.BlockSpec((1, tm, n_e), lambda b, i, e, f: (b, i, 0)),
                  pl.BlockSpec((None, d, tf), lambda b, i, e, f: (e, 0, f)),
                  pl.BlockSpec((None, d, tf), lambda b, i, e, f: (e, 0, f)),
                  pl.BlockSpec((None, tf, d), lambda b, i, e, f: (e, f, 0))],
        out_specs=tok,
        out_shape=jax.ShapeDtypeStruct((g, tg, d), F32),
        scratch_shapes=[pltpu.VMEM((tm, d), F32)],
        compiler_params=_cparams(("parallel", "parallel", "arbitrary", "arbitrary")),
        name="ffn_moe",
    )(h2, x, gate_f, gates, w1b, w3b, w2b)


def kernel(x_prompt, x_sample, cache_k, cache_v, state_conv, page_table, c_prompt, c_sample, w_mod, b_mod,
           norm1_g, norm2_g, w_in, q_norm_g, k_norm_g, w_o_attn, w_dw, b_dw, conv_ln_g, conv_ln_b, w_pw_conv,
           w_out, ffn_w1, ffn_w3, ffn_w2, router_w, router_b, moe_w1, moe_w3, moe_w2):
    depth = w_in.shape[0]
    n_p, t_p, d = x_prompt.shape
    n_s, t_s, _ = x_sample.shape
    n_tok_s = n_s * t_s
    n_phys = cache_k.shape[1]
    n_blk_p = t_p // MOBA_BLOCK

    c_all = jnp.concatenate([c_prompt, c_sample], axis=0)
    c_all = jnp.pad(c_all, ((0, (-c_all.shape[0]) % 8), (0, 0)))
    mod = _modulation(c_all, w_mod, b_mod)

    cache_kt = cache_k.transpose(0, 1, 3, 4, 2).reshape(depth, n_phys, D_ATTN, PAGE_SIZE)
    cache_vt = cache_v.transpose(0, 1, 3, 4, 2).reshape(depth, n_phys, D_ATTN, PAGE_SIZE)

    head_id = jnp.arange(D_ATTN) // HEAD_DIM
    head_mean = jnp.where(head_id[:, None] == head_id[None, :], 1.0 / HEAD_DIM, 0.0).astype(BF16)

    xp = x_prompt
    xs = x_sample.reshape(1, n_tok_s, d)
    k_p, v_p, conv_p, k_s, v_s, conv_s = [], [], [], [], [], []
    for l in range(depth):
        mod_p = mod[l, :n_p].reshape(n_p, N_MOD, 1, d)
        mod_s = jnp.repeat(mod[l, n_p:n_p + n_s].reshape(n_s, N_MOD, d), t_s, axis=0)
        mp = [mod_p[:, i] for i in range(N_MOD)]
        msm = [mod_s[None, :, i] for i in range(N_MOD)]

        w_in_b = w_in[l].astype(BF16)
        g1 = norm1_g[l].reshape(1, d)
        g2 = norm2_g[l].reshape(1, d)
        gq = jnp.tile(q_norm_g[l], N_HEADS).reshape(1, D_ATTN)
        gk = jnp.tile(k_norm_g[l], N_HEADS).reshape(1, D_ATTN)
        w_o_b = w_o_attn[l].astype(BF16)
        w_pw_b = w_pw_conv[l].astype(BF16)
        w_out_b = w_out[l].astype(BF16)
        ln_g = conv_ln_g[l].reshape(1, D_CONV)
        ln_b = conv_ln_b[l].reshape(1, D_CONV)
        is_moe = l % 2 == 1
        router = None
        if is_moe:
            rw = router_w[l // 2]
            rw_hi = rw.astype(BF16)
            rw_lo = (rw - rw_hi.astype(F32)).astype(BF16)
            router = (rw_hi, rw_lo, router_b[l // 2].reshape(1, N_EXPERTS))

        q_pad, k_aug, v_aug, kf, vf, u, ga, gc, km = _in_projection(
            xp, mp[0], mp[1], g1, w_in_b, head_mean, gq, gk, tm=MOBA_BLOCK, head_major=True)
        kmean = km.reshape(n_p, n_blk_p, N_HEADS, HEAD_DIM).transpose(0, 2, 1, 3)
        kmean = jnp.pad(kmean, ((0, 0), (0, 0), (0, 0), (0, LANES - HEAD_DIM)))
        attn = _moba_prompt(q_pad, k_aug, v_aug, kmean)
        outs = _post_mixer(xp, u, attn, ga, gc, mp[2], mp[3], mp[4], (w_dw[l], b_dw[l].reshape(1, D_CONV)),
                           ln_g, ln_b, w_pw_b, w_o_b, w_out_b, g2, router, tm=256)
        k_p.append(kf.reshape(n_p, t_p, N_HEADS, HEAD_DIM))
        v_p.append(vf.reshape(n_p, t_p, N_HEADS, HEAD_DIM))
        conv_p.append(u[:, t_p - (CONV_W - 1):])
        xp_mid, h2_p = outs[0], outs[1]
        gates_p = outs[2] if is_moe else None

        qs, kfs, vfs, us, gas, gcs = _in_projection(
            xs, msm[0], msm[1], g1, w_in_b, head_mean, gq, gk, tm=min(256, n_tok_s), head_major=False)
        seq = lambda a: a.reshape(n_s, t_s, a.shape[-1])
        attn_s = _moba_sample(page_table, seq(qs), seq(kfs), seq(vfs), cache_kt, cache_vt, l)
        attn_s = attn_s.reshape(1, n_tok_s, D_ATTN).astype(BF16)
        u_s = seq(us)
        y_dw = _conv_sample(state_conv[l], u_s, w_dw[l], b_dw[l]).reshape(1, n_tok_s, D_CONV)
        outs = _post_mixer(xs, y_dw, attn_s, gas, gcs, msm[2], msm[3], msm[4], None,
                           ln_g, ln_b, w_pw_b, w_o_b, w_out_b, g2, router, tm=min(256, n_tok_s))
        k_s.append(kfs.reshape(n_s, t_s, N_HEADS, HEAD_DIM))
        v_s.append(vfs.reshape(n_s, t_s, N_HEADS, HEAD_DIM))
        conv_s.append(jnp.concatenate([state_conv[l], u_s], axis=1)[:, -(CONV_W - 1):])
        xs_mid, h2_s = outs[0], outs[1]
        gates_s = outs[2] if is_moe else None

        if is_moe:
            w1b = moe_w1[l // 2].astype(BF16)
            w3b = moe_w3[l // 2].astype(BF16)
            w2b = moe_w2[l // 2].astype(BF16)
            xp = _ffn_moe(h2_p, xp_mid, mp[5], gates_p, w1b, w3b, w2b, tm=512, tf=1408)
            xs = _ffn_moe(h2_s, xs_mid, msm[5], gates_s, w1b, w3b, w2b, tm=512, tf=1408)
        else:
            w1b = ffn_w1[l // 2].astype(BF16)
            w3b = ffn_w3[l // 2].astype(BF16)
            w2b = ffn_w2[l // 2].astype(BF16)
            xp = _ffn_dense(h2_p, xp_mid, mp[5], w1b, w3b, w2b, tm=512, tf=1408)
            xs = _ffn_dense(h2_s, xs_mid, msm[5], w1b, w3b, w2b, tm=512, tf=1408)

    return (xp, xs.reshape(n_s, t_s, d), jnp.stack(k_p), jnp.stack(v_p), jnp.stack(conv_p),
            jnp.stack(k_s), jnp.stack(v_s), jnp.stack(conv_s))
```

```python
import functools

import jax
import jax.numpy as jnp
from jax import lax
from jax.experimental import pallas as pl
from jax.experimental.pallas import tpu as pltpu

F32 = jnp.float32
BF16 = jnp.bfloat16
HIGHEST = lax.Precision.HIGHEST

D_MODEL = 1024
N_HEADS = 8
HEAD_DIM = 64
D_ATTN = N_HEADS * HEAD_DIM
MOBA_BLOCK = 256
MOBA_TOPK = 3
D_CONV = 512
CONV_W = 31
N_EXPERTS = 8
N_MOD = 6
PAGE_SIZE = 128
NORM_EPS = 1e-6
LN_EPS = 1e-5
NEG = -1e30
LANES = 128
CONV_HALO = 32
VMEM_LIMIT = 56 * 1024 * 1024


def _cparams(sem):
    return pltpu.CompilerParams(dimension_semantics=sem, vmem_limit_bytes=VMEM_LIMIT)


def _dot(a, b):
    return jnp.dot(a, b, preferred_element_type=F32)


def _dot_nt(a, b, precision=None):
    return lax.dot_general(a, b, (((1,), (1,)), ((), ())), precision=precision, preferred_element_type=F32)


def _sigmoid(x):
    return 1.0 / (1.0 + jnp.exp(-x))


def _split_bf16(x):
    hi = x.astype(BF16)
    lo = (x - hi.astype(F32)).astype(BF16)
    return hi, lo


def _top_select(score, valid, idx, n_sel, axis, sentinel):
    remaining = valid
    sel = jnp.zeros(score.shape, dtype=jnp.bool_)
    for _ in range(n_sel):
        cur = jnp.where(remaining, score, -jnp.inf)
        best = jnp.max(cur, axis=axis, keepdims=True)
        cand = remaining & (cur == best)
        first = jnp.min(jnp.where(cand, idx, sentinel), axis=axis, keepdims=True)
        pick = idx == first
        sel = sel | pick
        remaining = remaining & jnp.logical_not(pick)
    return sel


def _mod_kernel(c_ref, w_ref, b_ref, o_ref):
    c = c_ref[...]
    cond = c * _sigmoid(c)
    o_ref[0] = _dot(cond.astype(BF16), w_ref[0].astype(BF16)) + b_ref[0]


def _modulation(c_all, w_mod, b_mod):
    depth, d, n = w_mod.shape
    rows = c_all.shape[0]
    tn = 1536
    return pl.pallas_call(
        _mod_kernel,
        grid=(depth, n // tn),
        in_specs=[
            pl.BlockSpec((rows, d), lambda l, j: (0, 0)),
            pl.BlockSpec((1, d, tn), lambda l, j: (l, 0, j)),
            pl.BlockSpec((1, 1, tn), lambda l, j: (l, 0, j)),
        ],
        out_specs=pl.BlockSpec((1, rows, tn), lambda l, j: (l, 0, j)),
        out_shape=jax.ShapeDtypeStruct((depth, rows, n), F32),
        compiler_params=_cparams(("parallel", "parallel")),
        name="adaln_modulation",
    )(c_all, w_mod, b_mod.reshape(depth, 1, n))


def _inproj_kernel(x_ref, shift_ref, scale_ref, g1_ref, w_ref, hm_ref, gq_ref, gk_ref, *out_refs, tm, head_major):
    if head_major:
        qp_ref, ka_ref, va_ref, kf_ref, vf_ref, u_ref, ga_ref, gc_ref, km_ref = out_refs
    else:
        qf_ref, kf_ref, vf_ref, u_ref, ga_ref, gc_ref = out_refs
    x = x_ref[0]
    ms = jnp.mean(x * x, axis=-1, keepdims=True)
    h = (x * lax.rsqrt(ms + NORM_EPS)) * g1_ref[...]
    h = h * (1.0 + scale_ref[0]) + shift_ref[0]
    hb = h.astype(BF16)

    def proj(lo, hi):
        return _dot(hb, w_ref[:, lo:hi])

    def head_rms(z, g_ref):
        z2_hi, z2_lo = _split_bf16(z * z)
        msh = _dot(z2_hi, hm_ref[...]) + _dot(z2_lo, hm_ref[...])
        return z * lax.rsqrt(msh + NORM_EPS) * g_ref[...]

    b0, b1, b2, b3, b4, b5, b6 = (0, D_ATTN, 2 * D_ATTN, 3 * D_ATTN, 3 * D_ATTN + D_CONV,
                                  3 * D_ATTN + 2 * D_CONV, 3 * D_ATTN + 2 * D_CONV + D_MODEL)
    q = head_rms(proj(b0, b1), gq_ref)
    k = head_rms(proj(b1, b2), gk_ref)
    v = proj(b2, b3)
    kf_ref[0] = k
    vf_ref[0] = v
    if head_major:
        blk = pl.program_id(1) * (tm // MOBA_BLOCK)
        lane = lax.broadcasted_iota(jnp.int32, (tm, LANES), 1)
        low = lane < HEAD_DIM
        k_tail = jnp.where(lane == HEAD_DIM + blk, 1.0, 0.0)
        v_tail = jnp.where(lane == HEAD_DIM, 1.0, 0.0)
        for hp in range(N_HEADS // 2):
            sl = slice(hp * LANES, (hp + 1) * LANES)
            for z, dst, tail in ((q, qp_ref, 0.0), (k, ka_ref, k_tail), (v, va_ref, v_tail)):
                pair = z[:, sl]
                dst[0, 2 * hp] = jnp.where(low, pair, tail).astype(dst.dtype)
                dst[0, 2 * hp + 1] = jnp.where(low, pltpu.roll(pair, HEAD_DIM, 1), tail).astype(dst.dtype)
        km_ref[0, 0] = jnp.mean(k, axis=0, keepdims=True)
    else:
        qf_ref[0] = q
    u_ref[0] = proj(b3, b4) * _sigmoid(proj(b4, b5))
    ga_ref[0] = _sigmoid(proj(b5, b6)).astype(BF16)
    gc_ref[0] = _sigmoid(proj(b6, b6 + D_MODEL)).astype(BF16)


def _in_projection(x, shift, scale, g1, w_in_b, head_mean, gq, gk, *, tm, head_major):
    g, tg, d = x.shape
    r = shift.shape[1]
    n_in = w_in_b.shape[1]
    if r == 1:
        mod_spec = pl.BlockSpec((1, 1, d), lambda b, i: (b, 0, 0))
    else:
        mod_spec = pl.BlockSpec((1, tm, d), lambda b, i: (b, i, 0))
    const2 = lambda b, i: (0, 0)
    tok = lambda w: pl.BlockSpec((1, tm, w), lambda b, i: (b, i, 0))
    tok_shape = lambda w, dt: jax.ShapeDtypeStruct((g, tg, w), dt)
    common_shapes = (tok_shape(D_ATTN, F32), tok_shape(D_ATTN, F32), tok_shape(D_CONV, F32),
                     tok_shape(d, BF16), tok_shape(d, BF16))
    common_specs = (tok(D_ATTN), tok(D_ATTN), tok(D_CONV), tok(d), tok(d))
    if head_major:
        assert tm == MOBA_BLOCK
        heads = pl.BlockSpec((1, N_HEADS, tm, LANES), lambda b, i: (b, 0, i, 0))
        hshape = lambda dt: jax.ShapeDtypeStruct((g, N_HEADS, tg, LANES), dt)
        out_shapes = (hshape(F32), hshape(BF16), hshape(BF16)) + common_shapes + (
            jax.ShapeDtypeStruct((g, tg // MOBA_BLOCK, 1, D_ATTN), F32),)
        out_specs = (heads, heads, heads) + common_specs + (
            pl.BlockSpec((1, 1, 1, D_ATTN), lambda b, i: (b, i, 0, 0)),)
    else:
        out_shapes = (tok_shape(D_ATTN, F32),) + common_shapes
        out_specs = (tok(D_ATTN),) + common_specs
    return pl.pallas_call(
        functools.partial(_inproj_kernel, tm=tm, head_major=head_major),
        grid=(g, tg // tm),
        in_specs=[tok(d), mod_spec, mod_spec,
                  pl.BlockSpec((1, d), const2),
                  pl.BlockSpec((d, n_in), const2),
                  pl.BlockSpec((D_ATTN, D_ATTN), const2),
                  pl.BlockSpec((1, D_ATTN), const2),
                  pl.BlockSpec((1, D_ATTN), const2)],
        out_specs=out_specs,
        out_shape=out_shapes,
        compiler_params=_cparams(("parallel", "parallel")),
        name="in_projection",
    )(x, shift, scale, g1, w_in_b, head_mean, gq, gk)


def _moba_prompt_kernel(q_ref, k_ref, v_ref, km_ref, o_ref, qa_scr, s_scr, m_scr, acc_scr, *, n_blk, hps):
    j = pl.program_id(2)
    tq = MOBA_BLOCK
    tk = 2 * MOBA_BLOCK
    own_pair = j // 2
    lane = lax.broadcasted_iota(jnp.int32, (tq, LANES), 1)
    blk_t = lax.broadcasted_iota(jnp.int32, (n_blk, tq), 0)

    def lane_max(s):
        m = s[:, :LANES]
        for c in range(1, tk // LANES):
            m = jnp.maximum(m, s[:, c * LANES:(c + 1) * LANES])
        return m

    for hh in range(hps):
        qp = q_ref[0, hh]
        gate_t = _dot_nt(km_ref[0, hh], qp, precision=HIGHEST)
        sel_t = _top_select(gate_t, blk_t < j, blk_t, min(MOBA_TOPK, n_blk), 0, n_blk)
        bias_t = jnp.where(sel_t | (blk_t == j), 0.0, NEG)
        bias_rows = jnp.concatenate([jnp.zeros((HEAD_DIM, tq), F32), bias_t,
                                     jnp.zeros((LANES - HEAD_DIM - n_blk, tq), F32)], axis=0)
        qa_scr[hh] = (qp * (HEAD_DIM ** -0.5) + jnp.transpose(bias_rows)).astype(BF16)

    own_off = (j & 1) * tq
    row = lax.broadcasted_iota(jnp.int32, (tq, tk), 0)
    col = lax.broadcasted_iota(jnp.int32, (tq, tk), 1)
    hidden = col - own_off > row
    st_own = pl.multiple_of(own_pair * tk, tk)
    for hh in range(hps):
        s = _dot_nt(qa_scr[hh], k_ref[0, hh, pl.ds(st_own, tk), :])
        s = jnp.where(hidden, NEG, s)
        s_scr[hh, own_pair] = s
        m_scr[hh] = lane_max(s)

    def pass1(ip, carry):
        st = pl.multiple_of(ip * tk, tk)
        for hh in range(hps):
            s = _dot_nt(qa_scr[hh], k_ref[0, hh, pl.ds(st, tk), :])
            s_scr[hh, ip] = s
            m_scr[hh] = jnp.maximum(m_scr[hh], lane_max(s))
        return carry

    lax.fori_loop(0, own_pair, pass1, 0)
    for hh in range(hps):
        m = jnp.max(m_scr[hh], axis=-1, keepdims=True)
        m_scr[hh] = jnp.broadcast_to(m, (tq, LANES))
        acc_scr[hh] = jnp.zeros((tq, LANES), F32)

    def pass2(ip, carry):
        st = pl.multiple_of(ip * tk, tk)
        for hh in range(hps):
            m = m_scr[hh]
            p = jnp.concatenate(
                [jnp.exp(s_scr[hh, ip, :, c * LANES:(c + 1) * LANES] - m).astype(BF16) for c in range(tk // LANES)],
                axis=-1)
            acc_scr[hh] += _dot(p, v_ref[0, hh, pl.ds(st, tk), :])
        return carry

    lax.fori_loop(0, own_pair + 1, pass2, 0)
    outs = []
    for hh in range(hps):
        acc = acc_scr[hh]
        denom = jnp.sum(jnp.where(lane == HEAD_DIM, acc, 0.0), axis=-1, keepdims=True)
        outs.append(acc / denom)
    slabs = [jnp.where(lane < HEAD_DIM, outs[2 * i], pltpu.roll(outs[2 * i + 1], HEAD_DIM, 1))
             for i in range(hps // 2)]
    o_ref[0] = jnp.concatenate(slabs, axis=-1).astype(o_ref.dtype)


def _moba_prompt(q_pad, k_aug, v_aug, kmean_pad, *, hps=4):
    b, h, t, _ = q_pad.shape
    n_blk = t // MOBA_BLOCK
    assert HEAD_DIM + n_blk <= LANES and n_blk % 2 == 0 and h % hps == 0 and hps % 2 == 0
    tq = MOBA_BLOCK
    return pl.pallas_call(
        functools.partial(_moba_prompt_kernel, n_blk=n_blk, hps=hps),
        grid=(b, h // hps, n_blk),
        in_specs=[pl.BlockSpec((1, hps, tq, LANES), lambda bi, hg, j: (bi, hg, j, 0)),
                  pl.BlockSpec((1, hps, t, LANES), lambda bi, hg, j: (bi, hg, 0, 0)),
                  pl.BlockSpec((1, hps, t, LANES), lambda bi, hg, j: (bi, hg, 0, 0)),
                  pl.BlockSpec((1, hps, n_blk, LANES), lambda bi, hg, j: (bi, hg, 0, 0))],
        out_specs=pl.BlockSpec((1, tq, hps * HEAD_DIM), lambda bi, hg, j: (bi, j, hg)),
        out_shape=jax.ShapeDtypeStruct((b, t, h * HEAD_DIM), BF16),
        scratch_shapes=[pltpu.VMEM((hps, tq, LANES), BF16),
                        pltpu.VMEM((hps, n_blk // 2, tq, 2 * tq), F32),
                        pltpu.VMEM((hps, tq, LANES), F32),
                        pltpu.VMEM((hps, tq, LANES), F32)],
        compiler_params=_cparams(("parallel", "parallel", "arbitrary")),
        name="moba_prompt_attention",
    )(q_pad, k_aug, v_aug, kmean_pad)


def _moba_sample_kernel(pt_ref, q_ref, knew_ref, vnew_ref, *refs, n_pages, n_q):
    del pt_ref
    k_refs = refs[:n_pages]
    v_refs = refs[n_pages:2 * n_pages]
    o_ref = refs[2 * n_pages]
    s_scr = refs[2 * n_pages + 1]
    n_row = n_q * N_HEADS
    pages_per_blk = MOBA_BLOCK // PAGE_SIZE
    n_blk = n_pages // pages_per_blk
    h_shift = N_HEADS.bit_length() - 1
    d_shift = HEAD_DIM.bit_length() - 1
    scale = HEAD_DIM ** -0.5

    q = q_ref[0]
    q_rep = jnp.concatenate([jnp.broadcast_to(q[i:i + 1, :], (N_HEADS, D_ATTN)) for i in range(n_q)], axis=0)
    r_i = lax.broadcasted_iota(jnp.int32, (n_row, D_ATTN), 0)
    c_i = lax.broadcasted_iota(jnp.int32, (n_row, D_ATTN), 1)
    own_head = (r_i & (N_HEADS - 1)) == (c_i >> d_shift)
    q_bd = jnp.where(own_head, q_rep, 0.0)
    q_hi, q_lo = _split_bf16(q_bd)

    lane = lax.broadcasted_iota(jnp.int32, (n_row, PAGE_SIZE), 1)
    gate = jnp.zeros((n_row, PAGE_SIZE), F32)
    for pg in range(n_pages):
        k_hi, k_lo = _split_bf16(k_refs[pg][...])
        s = _dot(q_hi, k_hi) + (_dot(q_lo, k_hi) + _dot(q_hi, k_lo))
        s_scr[pg] = s
        tot = jnp.sum(s, axis=-1, keepdims=True)
        gate = gate + jnp.where(lane == pg // pages_per_blk, tot, 0.0)
    sel = _top_select(gate * (1.0 / MOBA_BLOCK), lane < n_blk, lane, min(MOBA_TOPK, n_blk + 1), 1, PAGE_SIZE)
    sel_f = jnp.where(sel, 1.0, 0.0)

    knew = knew_ref[0]
    vnew = vnew_ref[0]
    q_idx = lax.broadcasted_iota(jnp.int32, (n_row, 1), 0) >> h_shift
    s_new = [jnp.where(q_idx >= jj, jnp.sum(q_bd * knew[jj:jj + 1, :], axis=-1, keepdims=True) * scale, NEG)
             for jj in range(n_q)]
    m = s_new[0]
    for jj in range(1, n_q):
        m = jnp.maximum(m, s_new[jj])
    chosen = [jnp.max(jnp.where(lane == bi, sel_f, 0.0), axis=-1, keepdims=True) > 0.0 for bi in range(n_blk)]
    mpart = jnp.full((n_row, PAGE_SIZE), NEG, F32)
    for pg in range(n_pages):
        s = jnp.where(chosen[pg // pages_per_blk], s_scr[pg] * scale, NEG)
        s_scr[pg] = s
        mpart = jnp.maximum(mpart, s)
    m = jnp.maximum(m, jnp.max(mpart, axis=-1, keepdims=True))

    acc = jnp.zeros((n_row, D_ATTN), F32)
    l = jnp.zeros((n_row, 1), F32)
    for jj in range(n_q):
        p = jnp.exp(s_new[jj] - m)
        l = l + p
        acc = acc + p * vnew[jj:jj + 1, :]
    lpart = jnp.zeros((n_row, PAGE_SIZE), F32)
    for pg in range(n_pages):
        p = jnp.exp(s_scr[pg] - m)
        lpart = lpart + p
        acc = acc + _dot_nt(p.astype(BF16), v_refs[pg][...].astype(BF16))
    l = l + jnp.sum(lpart, axis=-1, keepdims=True)
    out = jnp.where(own_head, acc / l, 0.0).reshape(n_q, N_HEADS, D_ATTN)
    o_ref[0] = jnp.sum(out, axis=1)


def _moba_sample(page_table, q, k_new, v_new, cache_kt, cache_vt, layer):
    n_seq, n_pages = page_table.shape
    n_q = q.shape[1]
    seq3 = pl.BlockSpec((1, n_q, D_ATTN), lambda b, pt: (b, 0, 0))

    def page_spec(pg):
        return pl.BlockSpec((None, None, D_ATTN, PAGE_SIZE), lambda b, pt, pg=pg: (layer, pt[b, pg], 0, 0))

    in_specs = [seq3, seq3, seq3] + [page_spec(pg) for pg in range(n_pages)] * 2
    grid_spec = pltpu.PrefetchScalarGridSpec(
        num_scalar_prefetch=1, grid=(n_seq,), in_specs=in_specs, out_specs=seq3,
        scratch_shapes=[pltpu.VMEM((n_pages, n_q * N_HEADS, PAGE_SIZE), F32)])
    return pl.pallas_call(
        functools.partial(_moba_sample_kernel, n_pages=n_pages, n_q=n_q),
        grid_spec=grid_spec,
        out_shape=jax.ShapeDtypeStruct((n_seq, n_q, D_ATTN), F32),
        compiler_params=_cparams(("arbitrary",)),
        name="moba_sample_attention",
    )(page_table, q, k_new, v_new, *([cache_kt] * n_pages), *([cache_vt] * n_pages))


def _conv_sample_kernel(st_ref, u_ref, wst_ref, wu_ref, b_ref, o_ref, *, n_q):
    st = st_ref[...]
    u = u_ref[...]
    for t in range(n_q):
        y = jnp.sum(st * wst_ref[t], axis=1) + jnp.sum(u * wu_ref[t], axis=1)
        o_ref[t] = y + b_ref[...]


def _conv_sample(state, u, w_dw, b_dw):
    n_seq, ctx, c = state.shape
    n_q = u.shape[1]
    r = jnp.arange(ctx)[None, :] - jnp.arange(n_q)[:, None]
    w_state = jnp.where((r >= 0)[..., None], w_dw[jnp.clip(r, 0, CONV_W - 1)], 0.0)
    ju = ctx + jnp.arange(n_q)[None, :] - jnp.arange(n_q)[:, None]
    w_new = jnp.where((ju <= ctx)[..., None], w_dw[jnp.clip(ju, 0, CONV_W - 1)], 0.0)
    sb = 16
    out = pl.pallas_call(
        functools.partial(_conv_sample_kernel, n_q=n_q),
        grid=(n_seq // sb,),
        in_specs=[pl.BlockSpec((sb, ctx, c), lambda i: (i, 0, 0)),
                  pl.BlockSpec((sb, n_q, c), lambda i: (i, 0, 0)),
                  pl.BlockSpec((n_q, ctx, c), lambda i: (0, 0, 0)),
                  pl.BlockSpec((n_q, n_q, c), lambda i: (0, 0, 0)),
                  pl.BlockSpec((1, c), lambda i: (0, 0))],
        out_specs=pl.BlockSpec((n_q, sb, c), lambda i: (0, i, 0)),
        out_shape=jax.ShapeDtypeStruct((n_q, n_seq, c), F32),
        compiler_params=_cparams(("parallel",)),
        name="conv_sample",
    )(state, u, w_state, w_new, b_dw.reshape(1, c))
    return out.transpose(1, 0, 2)


def _post_kernel(*refs, tm, do_conv, do_route):
    it = iter(refs)
    x_ref, u_ref = next(it), next(it)
    halo_ref = next(it) if do_conv else None
    attn_ref, ga_ref, gc_ref, gm_ref, sf_ref, scf_ref = (next(it) for _ in range(6))
    if do_conv:
        wdw_ref, bdw_ref = next(it), next(it)
    lng_ref, lnb_ref, wpw_ref, wo_ref, wout_ref, g2_ref = (next(it) for _ in range(6))
    if do_route:
        rwh_ref, rwl_ref, rb_ref = next(it), next(it), next(it)
    xo_ref, h2_ref = next(it), next(it)
    gate_ref = next(it) if do_route else None
    if do_conv:
        xbuf, ybuf = next(it), next(it)

    if do_conv:
        i = pl.program_id(1)
        xbuf[0:CONV_HALO, :] = jnp.where(i > 0, halo_ref[0], 0.0)
        xbuf[CONV_HALO:CONV_HALO + tm, :] = u_ref[0]
        off = CONV_HALO - (CONV_W - 1)
        for c0 in range(0, D_CONV, LANES):
            acc = jnp.broadcast_to(bdw_ref[:, c0:c0 + LANES], (tm, LANES))
            for w in range(CONV_W):
                acc = acc + xbuf[off + w:off + w + tm, c0:c0 + LANES] * wdw_ref[w:w + 1, c0:c0 + LANES]
            ybuf[:, c0:c0 + LANES] = acc
        yc = ybuf[...]
    else:
        yc = u_ref[0]

    mu = jnp.mean(yc, axis=-1, keepdims=True)
    var = jnp.mean(jnp.square(yc - mu), axis=-1, keepdims=True)
    yn = (yc - mu) * lax.rsqrt(var + LN_EPS) * lng_ref[...] + lnb_ref[...]
    y_conv = _dot((yn * _sigmoid(yn)).astype(BF16), wpw_ref[...])
    y_attn = _dot(attn_ref[0], wo_ref[...])
    merged = ga_ref[0].astype(F32) * y_attn + gc_ref[0].astype(F32) * y_conv
    xn = x_ref[0] + gm_ref[0] * _dot(merged.astype(BF16), wout_ref[...])
    xo_ref[0] = xn
    ms = jnp.mean(xn * xn, axis=-1, keepdims=True)
    h2 = (xn * lax.rsqrt(ms + NORM_EPS)) * g2_ref[...]
    h2 = h2 * (1.0 + scf_ref[0]) + sf_ref[0]
    h2_ref[0] = h2.astype(h2_ref.dtype)
    if do_route:
        h_hi, h_lo = _split_bf16(h2)
        logits = (_dot(h_hi, rwh_ref[...]) + (_dot(h_lo, rwh_ref[...]) + _dot(h_hi, rwl_ref[...]))) + rb_ref[...]
        eidx = lax.broadcasted_iota(jnp.int32, logits.shape, 1)
        top1 = jnp.max(logits, axis=-1, keepdims=True)
        i1 = jnp.min(jnp.where(logits == top1, eidx, N_EXPERTS), axis=-1, keepdims=True)
        rest = jnp.where(eidx == i1, -jnp.inf, logits)
        top2 = jnp.max(rest, axis=-1, keepdims=True)
        i2 = jnp.min(jnp.where(rest == top2, eidx, N_EXPERTS), axis=-1, keepdims=True)
        e2 = jnp.exp(top2 - top1)
        w1 = 1.0 / (1.0 + e2)
        w2 = e2 / (1.0 + e2)
        gate_ref[0] = jnp.where(eidx == i1, w1, 0.0) + jnp.where(eidx == i2, w2, 0.0)


def _post_mixer(x, u, attn, ga, gc, gate_m, shift_f, scale_f, conv_w, ln_g, ln_b, w_pw_b, w_o_b, w_out_b, g2,
                router, *, tm):
    g, tg, d = x.shape
    r = gate_m.shape[1]
    do_conv = conv_w is not None
    do_route = router is not None
    if r == 1:
        mod_spec = pl.BlockSpec((1, 1, d), lambda b, i: (b, 0, 0))
    else:
        mod_spec = pl.BlockSpec((1, tm, d), lambda b, i: (b, i, 0))
    tok = lambda w: pl.BlockSpec((1, tm, w), lambda b, i: (b, i, 0))
    const2 = lambda b, i: (0, 0)
    args = [x, u]
    specs = [tok(d), tok(D_CONV)]
    if do_conv:
        per = tm // CONV_HALO
        args.append(u)
        specs.append(pl.BlockSpec((1, CONV_HALO, D_CONV), lambda b, i: (b, jnp.maximum(i * per - 1, 0), 0)))
    args += [attn, ga, gc, gate_m, shift_f, scale_f]
    specs += [tok(D_ATTN), tok(d), tok(d), mod_spec, mod_spec, mod_spec]
    if do_conv:
        args += [conv_w[0], conv_w[1]]
        specs += [pl.BlockSpec((CONV_W, D_CONV), const2), pl.BlockSpec((1, D_CONV), const2)]
    args += [ln_g, ln_b, w_pw_b, w_o_b, w_out_b, g2]
    specs += [pl.BlockSpec((1, D_CONV), const2), pl.BlockSpec((1, D_CONV), const2),
              pl.BlockSpec((D_CONV, d), const2), pl.BlockSpec((D_ATTN, d), const2),
              pl.BlockSpec((d, d), const2), pl.BlockSpec((1, d), const2)]
    if do_route:
        args += list(router)
        specs += [pl.BlockSpec((d, N_EXPERTS), const2), pl.BlockSpec((d, N_EXPERTS), const2),
                  pl.BlockSpec((1, N_EXPERTS), const2)]
    out_shape = [jax.ShapeDtypeStruct((g, tg, d), F32), jax.ShapeDtypeStruct((g, tg, d), F32 if do_route else BF16)]
    out_specs = [tok(d), tok(d)]
    if do_route:
        out_shape.append(jax.ShapeDtypeStruct((g, tg, N_EXPERTS), F32))
        out_specs.append(tok(N_EXPERTS))
    scratch = []
    if do_conv:
        scratch = [pltpu.VMEM((CONV_HALO + tm, D_CONV), F32), pltpu.VMEM((tm, D_CONV), F32)]
    outs = pl.pallas_call(
        functools.partial(_post_kernel, tm=tm, do_conv=do_conv, do_route=do_route),
        grid=(g, tg // tm),
        in_specs=specs,
        out_specs=tuple(out_specs),
        out_shape=tuple(out_shape),
        scratch_shapes=scratch,
        compiler_params=_cparams(("parallel", "arbitrary")),
        name="post_mixer",
    )(*args)
    return outs


def _ffn_kernel(h_ref, x_ref, gf_ref, w1_ref, w3_ref, w2_ref, o_ref, acc_ref):
    f = pl.program_id(2)

    @pl.when(f == 0)
    def _():
        acc_ref[...] = jnp.zeros_like(acc_ref)

    hb = h_ref[0]
    a1 = _dot(hb, w1_ref[...])
    a3 = _dot(hb, w3_ref[...])
    act = (a1 * _sigmoid(a1) * a3).astype(BF16)
    acc_ref[...] += _dot(act, w2_ref[...])

    @pl.when(f == pl.num_programs(2) - 1)
    def _():
        o_ref[0] = x_ref[0] + gf_ref[0] * acc_ref[...]


def _ffn_dense(h2, x, gate_f, w1b, w3b, w2b, *, tm, tf):
    g, tg, d = x.shape
    dff = w1b.shape[1]
    tm = min(tm, tg)
    r = gate_f.shape[1]
    if r == 1:
        mod_spec = pl.BlockSpec((1, 1, d), lambda b, i, f: (b, 0, 0))
    else:
        mod_spec = pl.BlockSpec((1, tm, d), lambda b, i, f: (b, i, 0))
    tok = pl.BlockSpec((1, tm, d), lambda b, i, f: (b, i, 0))
    return pl.pallas_call(
        _ffn_kernel,
        grid=(g, tg // tm, dff // tf),
        in_specs=[tok, tok, mod_spec,
                  pl.BlockSpec((d, tf), lambda b, i, f: (0, f)),
                  pl.BlockSpec((d, tf), lambda b, i, f: (0, f)),
                  pl.BlockSpec((tf, d), lambda b, i, f: (f, 0))],
        out_specs=tok,
        out_shape=jax.ShapeDtypeStruct((g, tg, d), F32),
        scratch_shapes=[pltpu.VMEM((tm, d), F32)],
        compiler_params=_cparams(("parallel", "parallel", "arbitrary")),
        name="ffn_dense",
    )(h2, x, gate_f, w1b, w3b, w2b)


MOE_TILE = 256


def _row_gather(src_hbm, dst, sem, idx_ref, base, n_rows, *, start):
    def body(r, carry):
        cp = pltpu.make_async_copy(src_hbm.at[pl.ds(idx_ref[base + r], 1)], dst.at[pl.ds(r, 1)], sem)
        if start:
            cp.start()
        else:
            cp.wait()
        return carry
    lax.fori_loop(0, n_rows, body, 0)


def _moe_expert_kernel(tok_ref, exp_ref, act_ref, h_hbm, sg_ref, w1_ref, w3_ref, w2_ref, o_ref, xbuf, sem, *, tf):
    del exp_ref
    i = pl.program_id(0)
    n = pl.num_programs(0)
    slot = i % 2

    @pl.when(i == 0)
    def _():
        _row_gather(h_hbm, xbuf.at[0], sem.at[0], tok_ref, 0, MOE_TILE, start=True)

    @pl.when(i + 1 < n)
    def _():
        _row_gather(h_hbm, xbuf.at[1 - slot], sem.at[1 - slot], tok_ref, (i + 1) * MOE_TILE, MOE_TILE, start=True)

    _row_gather(h_hbm, xbuf.at[slot], sem.at[slot], tok_ref, i * MOE_TILE, MOE_TILE, start=False)

    @pl.when(act_ref[i] > 0)
    def _():
        xb = xbuf[slot].astype(BF16)
        dff = w1_ref.shape[1]
        for f0 in range(0, dff, tf):
            a1 = _dot(xb, w1_ref[:, f0:f0 + tf])
            a3 = _dot(xb, w3_ref[:, f0:f0 + tf])
            part = _dot((a1 * _sigmoid(a1) * a3).astype(BF16), w2_ref[f0:f0 + tf, :])
            if f0 == 0:
                o_ref[...] = part
            else:
                o_ref[...] += part
        o_ref[...] = sg_ref[...] * o_ref[...]

    @pl.when(act_ref[i] == 0)
    def _():
        o_ref[...] = jnp.zeros(o_ref.shape, F32)


def _moe_combine_kernel(d1_ref, d2_ref, y_hbm, x_ref, gf_ref, o_ref, ybuf, sem, *, tm):
    i = pl.program_id(0)
    n = pl.num_programs(0)
    slot = i % 2

    def both(tile, sl, start):
        _row_gather(y_hbm, ybuf.at[sl, 0], sem.at[sl], d1_ref, tile * tm, tm, start=start)
        _row_gather(y_hbm, ybuf.at[sl, 1], sem.at[sl], d2_ref, tile * tm, tm, start=start)

    @pl.when(i == 0)
    def _():
        both(0, 0, True)

    @pl.when(i + 1 < n)
    def _():
        both(i + 1, 1 - slot, True)

    both(i, slot, False)
    o_ref[...] = x_ref[...] + gf_ref[...] * (ybuf[slot, 0] + ybuf[slot, 1])


def _ffn_moe_routed(h2, x, gate_f, gates, w1b, w3b, w2b, *, tf):
    t, d = x.shape
    n_e, _, dff = w1b.shape
    tile = MOE_TILE
    n_tiles = (2 * t) // tile + n_e + 1
    s_pad = n_tiles * tile

    mask = gates > 0.0
    cnt = jnp.sum(mask.astype(jnp.int32), axis=0)
    gsz = ((cnt + tile - 1) // tile) * tile
    gend = jnp.cumsum(gsz)
    rank = jnp.cumsum(mask.astype(jnp.int32), axis=0) - 1
    dest = jnp.where(mask, (gend - gsz)[None, :] + rank, s_pad)
    tok = jnp.broadcast_to(jnp.arange(t, dtype=jnp.int32)[:, None], dest.shape)
    slot_tok = jnp.zeros((s_pad,), jnp.int32).at[dest.ravel()].set(tok.ravel(), mode="drop")
    slot_gate = jnp.zeros((s_pad,), F32).at[dest.ravel()].set(gates.ravel(), mode="drop").reshape(s_pad, 1)
    tile_start = jnp.arange(n_tiles, dtype=jnp.int32) * tile
    tile_exp = jnp.minimum(jnp.searchsorted(gend, tile_start, side="right"), n_e - 1).astype(jnp.int32)
    tile_act = (tile_start < gend[-1]).astype(jnp.int32)
    two = jnp.sort(dest, axis=1)[:, :2]
    two = jnp.where(two >= s_pad, s_pad - 1, two).astype(jnp.int32)
    d1, d2 = two[:, 0], two[:, 1]

    wspec = lambda shp, imap: pl.BlockSpec((None,) + shp, imap)
    y = pl.pallas_call(
        functools.partial(_moe_expert_kernel, tf=tf),
        grid_spec=pltpu.PrefetchScalarGridSpec(
            num_scalar_prefetch=3, grid=(n_tiles,),
            in_specs=[pl.BlockSpec(memory_space=pl.ANY),
                      pl.BlockSpec((tile, 1), lambda i, tk, ex, ac: (i, 0)),
                      wspec((d, dff), lambda i, tk, ex, ac: (ex[i], 0, 0)),
                      wspec((d, dff), lambda i, tk, ex, ac: (ex[i], 0, 0)),
                      wspec((dff, d), lambda i, tk, ex, ac: (ex[i], 0, 0))],
            out_specs=pl.BlockSpec((tile, d), lambda i, tk, ex, ac: (i, 0)),
            scratch_shapes=[pltpu.VMEM((2, tile, d), F32), pltpu.SemaphoreType.DMA((2,))]),
        out_shape=jax.ShapeDtypeStruct((s_pad, d), F32),
        compiler_params=_cparams(("arbitrary",)),
        name="moe_experts",
    )(slot_tok, tile_exp, tile_act, h2, slot_gate, w1b, w3b, w2b)

    tm = next(c for c in (256, 128, 64, 32, 16, 8) if t % c == 0)
    return pl.pallas_call(
        functools.partial(_moe_combine_kernel, tm=tm),
        grid_spec=pltpu.PrefetchScalarGridSpec(
            num_scalar_prefetch=2, grid=(t // tm,),
            in_specs=[pl.BlockSpec(memory_space=pl.ANY),
                      pl.BlockSpec((tm, d), lambda i, a, b: (i, 0)),
                      pl.BlockSpec((tm, d), lambda i, a, b: (i, 0))],
            out_specs=pl.BlockSpec((tm, d), lambda i, a, b: (i, 0)),
            scratch_shapes=[pltpu.VMEM((2, 2, tm, d), F32), pltpu.SemaphoreType.DMA((2,))]),
        out_shape=jax.ShapeDtypeStruct((t, d), F32),
        compiler_params=_cparams(("arbitrary",)),
        name="moe_combine",
    )(d1, d2, y, x, gate_f)


def kernel(x_prompt, x_sample, cache_k, cache_v, state_conv, page_table, c_prompt, c_sample, w_mod, b_mod,
           norm1_g, norm2_g, w_in, q_norm_g, k_norm_g, w_o_attn, w_dw, b_dw, conv_ln_g, conv_ln_b, w_pw_conv,
           w_out, ffn_w1, ffn_w3, ffn_w2, router_w, router_b, moe_w1, moe_w3, moe_w2):
    depth = w_in.shape[0]
    n_p, t_p, d = x_prompt.shape
    n_s, t_s, _ = x_sample.shape
    n_tok_s = n_s * t_s
    n_phys = cache_k.shape[1]
    n_blk_p = t_p // MOBA_BLOCK

    c_all = jnp.concatenate([c_prompt, c_sample], axis=0)
    c_all = jnp.pad(c_all, ((0, (-c_all.shape[0]) % 8), (0, 0)))
    mod = _modulation(c_all, w_mod, b_mod)

    cache_kt = cache_k.transpose(0, 1, 3, 4, 2).reshape(depth, n_phys, D_ATTN, PAGE_SIZE)
    cache_vt = cache_v.transpose(0, 1, 3, 4, 2).reshape(depth, n_phys, D_ATTN, PAGE_SIZE)

    head_id = jnp.arange(D_ATTN) // HEAD_DIM
    head_mean = jnp.where(head_id[:, None] == head_id[None, :], 1.0 / HEAD_DIM, 0.0).astype(BF16)

    xp = x_prompt
    xs = x_sample.reshape(1, n_tok_s, d)
    k_p, v_p, conv_p, k_s, v_s, conv_s = [], [], [], [], [], []
    for l in range(depth):
        mod_p = mod[l, :n_p].reshape(n_p, N_MOD, 1, d)
        mod_s = jnp.repeat(mod[l, n_p:n_p + n_s].reshape(n_s, N_MOD, d), t_s, axis=0)
        mp = [mod_p[:, i] for i in range(N_MOD)]
        msm = [mod_s[None, :, i] for i in range(N_MOD)]

        w_in_b = w_in[l].astype(BF16)
        g1 = norm1_g[l].reshape(1, d)
        g2 = norm2_g[l].reshape(1, d)
        gq = jnp.tile(q_norm_g[l], N_HEADS).reshape(1, D_ATTN)
        gk = jnp.tile(k_norm_g[l], N_HEADS).reshape(1, D_ATTN)
        w_o_b = w_o_attn[l].astype(BF16)
        w_pw_b = w_pw_conv[l].astype(BF16)
        w_out_b = w_out[l].astype(BF16)
        ln_g = conv_ln_g[l].reshape(1, D_CONV)
        ln_b = conv_ln_b[l].reshape(1, D_CONV)
        is_moe = l % 2 == 1
        router = None
        if is_moe:
            rw = router_w[l // 2]
            rw_hi = rw.astype(BF16)
            rw_lo = (rw - rw_hi.astype(F32)).astype(BF16)
            router = (rw_hi, rw_lo, router_b[l // 2].reshape(1, N_EXPERTS))

        q_pad, k_aug, v_aug, kf, vf, u, ga, gc, km = _in_projection(
            xp, mp[0], mp[1], g1, w_in_b, head_mean, gq, gk, tm=MOBA_BLOCK, head_major=True)
        kmean = km.reshape(n_p, n_blk_p, N_HEADS, HEAD_DIM).transpose(0, 2, 1, 3)
        kmean = jnp.pad(kmean, ((0, 0), (0, 0), (0, 0), (0, LANES - HEAD_DIM)))
        attn = _moba_prompt(q_pad, k_aug, v_aug, kmean)
        outs = _post_mixer(xp, u, attn, ga, gc, mp[2], mp[3], mp[4], (w_dw[l], b_dw[l].reshape(1, D_CONV)),
                           ln_g, ln_b, w_pw_b, w_o_b, w_out_b, g2, router, tm=256)
        k_p.append(kf.reshape(n_p, t_p, N_HEADS, HEAD_DIM))
        v_p.append(vf.reshape(n_p, t_p, N_HEADS, HEAD_DIM))
        conv_p.append(u[:, t_p - (CONV_W - 1):])
        xp_mid, h2_p = outs[0], outs[1]
        gates_p = outs[2] if is_moe else None

        qs, kfs, vfs, us, gas, gcs = _in_projection(
            xs, msm[0], msm[1], g1, w_in_b, head_mean, gq, gk, tm=min(256, n_tok_s), head_major=False)
        seq = lambda a: a.reshape(n_s, t_s, a.shape[-1])
        attn_s = _moba_sample(page_table, seq(qs), seq(kfs), seq(vfs), cache_kt, cache_vt, l)
        attn_s = attn_s.reshape(1, n_tok_s, D_ATTN).astype(BF16)
        u_s = seq(us)
        y_dw = _conv_sample(state_conv[l], u_s, w_dw[l], b_dw[l]).reshape(1, n_tok_s, D_CONV)
        outs = _post_mixer(xs, y_dw, attn_s, gas, gcs, msm[2], msm[3], msm[4], None,
                           ln_g, ln_b, w_pw_b, w_o_b, w_out_b, g2, router, tm=min(256, n_tok_s))
        k_s.append(kfs.reshape(n_s, t_s, N_HEADS, HEAD_DIM))
        v_s.append(vfs.reshape(n_s, t_s, N_HEADS, HEAD_DIM))
        conv_s.append(jnp.concatenate([state_conv[l], u_s], axis=1)[:, -(CONV_W - 1):])
        xs_mid, h2_s = outs[0], outs[1]
        gates_s = outs[2] if is_moe else None

        if is_moe:
            w1b = moe_w1[l // 2].astype(BF16)
            w3b = moe_w3[l // 2].astype(BF16)
            w2b = moe_w2[l // 2].astype(BF16)
            n_tok_p = n_p * t_p
            flat = lambda a_p, a_s: jnp.concatenate([a_p.reshape(n_tok_p, -1), a_s.reshape(n_tok_s, -1)], axis=0)
            gf_tok = flat(jnp.broadcast_to(mp[5], (n_p, t_p, d)), msm[5])
            x_all = _ffn_moe_routed(flat(h2_p, h2_s), flat(xp_mid, xs_mid), gf_tok, flat(gates_p, gates_s),
                                    w1b, w3b, w2b, tf=1408)
            xp = x_all[:n_tok_p].reshape(n_p, t_p, d)
            xs = x_all[n_tok_p:].reshape(1, n_tok_s, d)
        else:
            w1b = ffn_w1[l // 2].astype(BF16)
            w3b = ffn_w3[l // 2].astype(BF16)
            w2b = ffn_w2[l // 2].astype(BF16)
            xp = _ffn_dense(h2_p, xp_mid, mp[5], w1b, w3b, w2b, tm=512, tf=1408)
            xs = _ffn_dense(h2_s, xs_mid, msm[5], w1b, w3b, w2b, tm=512, tf=1408)

    return (xp, xs.reshape(n_s, t_s, d), jnp.stack(k_p), jnp.stack(v_p), jnp.stack(conv_p),
            jnp.stack(k_s), jnp.stack(v_s), jnp.stack(conv_s))
```

```python
import functools

import jax
import jax.numpy as jnp
from jax import lax
from jax.experimental import pallas as pl
from jax.experimental.pallas import tpu as pltpu

F32 = jnp.float32
BF16 = jnp.bfloat16
HIGHEST = lax.Precision.HIGHEST

D_MODEL = 1024
N_HEADS = 8
HEAD_DIM = 64
D_ATTN = N_HEADS * HEAD_DIM
MOBA_BLOCK = 256
MOBA_TOPK = 3
D_CONV = 512
CONV_W = 31
N_EXPERTS = 8
N_MOD = 6
PAGE_SIZE = 128
NORM_EPS = 1e-6
LN_EPS = 1e-5
NEG = -1e30
LANES = 128
CONV_HALO = 32
VMEM_LIMIT = 56 * 1024 * 1024


def _cparams(sem):
    return pltpu.CompilerParams(dimension_semantics=sem, vmem_limit_bytes=VMEM_LIMIT)


def _dot(a, b):
    return jnp.dot(a, b, preferred_element_type=F32)


def _dot_nt(a, b, precision=None):
    return lax.dot_general(a, b, (((1,), (1,)), ((), ())), precision=precision, preferred_element_type=F32)


def _sigmoid(x):
    return 1.0 / (1.0 + jnp.exp(-x))


def _split_bf16(x):
    hi = x.astype(BF16)
    lo = (x - hi.astype(F32)).astype(BF16)
    return hi, lo


def _top_select(score, valid, idx, n_sel, axis, sentinel):
    remaining = valid
    sel = jnp.zeros(score.shape, dtype=jnp.bool_)
    for _ in range(n_sel):
        cur = jnp.where(remaining, score, -jnp.inf)
        best = jnp.max(cur, axis=axis, keepdims=True)
        cand = remaining & (cur == best)
        first = jnp.min(jnp.where(cand, idx, sentinel), axis=axis, keepdims=True)
        pick = idx == first
        sel = sel | pick
        remaining = remaining & jnp.logical_not(pick)
    return sel


def _mod_kernel(c_ref, w_ref, b_ref, o_ref):
    c = c_ref[...]
    cond = c * _sigmoid(c)
    o_ref[0] = _dot(cond.astype(BF16), w_ref[0].astype(BF16)) + b_ref[0]


def _modulation(c_all, w_mod, b_mod):
    depth, d, n = w_mod.shape
    rows = c_all.shape[0]
    tn = 1536
    return pl.pallas_call(
        _mod_kernel,
        grid=(depth, n // tn),
        in_specs=[
            pl.BlockSpec((rows, d), lambda l, j: (0, 0)),
            pl.BlockSpec((1, d, tn), lambda l, j: (l, 0, j)),
            pl.BlockSpec((1, 1, tn), lambda l, j: (l, 0, j)),
        ],
        out_specs=pl.BlockSpec((1, rows, tn), lambda l, j: (l, 0, j)),
        out_shape=jax.ShapeDtypeStruct((depth, rows, n), F32),
        compiler_params=_cparams(("parallel", "parallel")),
        name="adaln_modulation",
    )(c_all, w_mod, b_mod.reshape(depth, 1, n))


def _inproj_kernel(x_ref, shift_ref, scale_ref, g1_ref, w_ref, hm_ref, gq_ref, gk_ref, *out_refs, tm, head_major):
    if head_major:
        qp_ref, ka_ref, va_ref, kf_ref, vf_ref, u_ref, ga_ref, gc_ref, km_ref = out_refs
    else:
        qf_ref, kf_ref, vf_ref, u_ref, ga_ref, gc_ref = out_refs
    x = x_ref[0]
    ms = jnp.mean(x * x, axis=-1, keepdims=True)
    h = (x * lax.rsqrt(ms + NORM_EPS)) * g1_ref[...]
    h = h * (1.0 + scale_ref[0]) + shift_ref[0]
    hb = h.astype(BF16)

    def proj(lo, hi):
        return _dot(hb, w_ref[:, lo:hi])

    def head_rms(z, g_ref):
        z2_hi, z2_lo = _split_bf16(z * z)
        msh = _dot(z2_hi, hm_ref[...]) + _dot(z2_lo, hm_ref[...])
        return z * lax.rsqrt(msh + NORM_EPS) * g_ref[...]

    b0, b1, b2, b3, b4, b5, b6 = (0, D_ATTN, 2 * D_ATTN, 3 * D_ATTN, 3 * D_ATTN + D_CONV,
                                  3 * D_ATTN + 2 * D_CONV, 3 * D_ATTN + 2 * D_CONV + D_MODEL)
    q = head_rms(proj(b0, b1), gq_ref)
    k = head_rms(proj(b1, b2), gk_ref)
    v = proj(b2, b3)
    kf_ref[0] = k
    vf_ref[0] = v
    if head_major:
        blk = pl.program_id(1) * (tm // MOBA_BLOCK)
        lane = lax.broadcasted_iota(jnp.int32, (tm, LANES), 1)
        low = lane < HEAD_DIM
        k_tail = jnp.where(lane == HEAD_DIM + blk, 1.0, 0.0)
        v_tail = jnp.where(lane == HEAD_DIM, 1.0, 0.0)
        for hp in range(N_HEADS // 2):
            sl = slice(hp * LANES, (hp + 1) * LANES)
            for z, dst, tail in ((q, qp_ref, 0.0), (k, ka_ref, k_tail), (v, va_ref, v_tail)):
                pair = z[:, sl]
                dst[0, 2 * hp] = jnp.where(low, pair, tail).astype(dst.dtype)
                dst[0, 2 * hp + 1] = jnp.where(low, pltpu.roll(pair, HEAD_DIM, 1), tail).astype(dst.dtype)
        km_ref[0, 0] = jnp.mean(k, axis=0, keepdims=True)
    else:
        qf_ref[0] = q
    u_ref[0] = proj(b3, b4) * _sigmoid(proj(b4, b5))
    ga_ref[0] = _sigmoid(proj(b5, b6)).astype(BF16)
    gc_ref[0] = _sigmoid(proj(b6, b6 + D_MODEL)).astype(BF16)


def _in_projection(x, shift, scale, g1, w_in_b, head_mean, gq, gk, *, tm, head_major):
    g, tg, d = x.shape
    r = shift.shape[1]
    n_in = w_in_b.shape[1]
    if r == 1:
        mod_spec = pl.BlockSpec((1, 1, d), lambda b, i: (b, 0, 0))
    else:
        mod_spec = pl.BlockSpec((1, tm, d), lambda b, i: (b, i, 0))
    const2 = lambda b, i: (0, 0)
    tok = lambda w: pl.BlockSpec((1, tm, w), lambda b, i: (b, i, 0))
    tok_shape = lambda w, dt: jax.ShapeDtypeStruct((g, tg, w), dt)
    common_shapes = (tok_shape(D_ATTN, F32), tok_shape(D_ATTN, F32), tok_shape(D_CONV, F32),
                     tok_shape(d, BF16), tok_shape(d, BF16))
    common_specs = (tok(D_ATTN), tok(D_ATTN), tok(D_CONV), tok(d), tok(d))
    if head_major:
        assert tm == MOBA_BLOCK
        heads = pl.BlockSpec((1, N_HEADS, tm, LANES), lambda b, i: (b, 0, i, 0))
        hshape = lambda dt: jax.ShapeDtypeStruct((g, N_HEADS, tg, LANES), dt)
        out_shapes = (hshape(F32), hshape(BF16), hshape(BF16)) + common_shapes + (
            jax.ShapeDtypeStruct((g, tg // MOBA_BLOCK, 1, D_ATTN), F32),)
        out_specs = (heads, heads, heads) + common_specs + (
            pl.BlockSpec((1, 1, 1, D_ATTN), lambda b, i: (b, i, 0, 0)),)
    else:
        out_shapes = (tok_shape(D_ATTN, F32),) + common_shapes
        out_specs = (tok(D_ATTN),) + common_specs
    return pl.pallas_call(
        functools.partial(_inproj_kernel, tm=tm, head_major=head_major),
        grid=(g, tg // tm),
        in_specs=[tok(d), mod_spec, mod_spec,
                  pl.BlockSpec((1, d), const2),
                  pl.BlockSpec((d, n_in), const2),
                  pl.BlockSpec((D_ATTN, D_ATTN), const2),
                  pl.BlockSpec((1, D_ATTN), const2),
                  pl.BlockSpec((1, D_ATTN), const2)],
        out_specs=out_specs,
        out_shape=out_shapes,
        compiler_params=_cparams(("parallel", "parallel")),
        name="in_projection",
    )(x, shift, scale, g1, w_in_b, head_mean, gq, gk)


def _moba_prompt_kernel(q_ref, k_ref, v_ref, km_ref, o_ref, qa_scr, s_scr, m_scr, acc_scr, *, n_blk, hps):
    j = pl.program_id(2)
    tq = MOBA_BLOCK
    tk = 2 * MOBA_BLOCK
    own_pair = j // 2
    lane = lax.broadcasted_iota(jnp.int32, (tq, LANES), 1)
    blk_t = lax.broadcasted_iota(jnp.int32, (n_blk, tq), 0)

    def lane_max(s):
        m = s[:, :LANES]
        for c in range(1, tk // LANES):
            m = jnp.maximum(m, s[:, c * LANES:(c + 1) * LANES])
        return m

    for hh in range(hps):
        qp = q_ref[0, hh]
        gate_t = _dot_nt(km_ref[0, hh], qp, precision=HIGHEST)
        sel_t = _top_select(gate_t, blk_t < j, blk_t, min(MOBA_TOPK, n_blk), 0, n_blk)
        bias_t = jnp.where(sel_t | (blk_t == j), 0.0, NEG)
        bias_rows = jnp.concatenate([jnp.zeros((HEAD_DIM, tq), F32), bias_t,
                                     jnp.zeros((LANES - HEAD_DIM - n_blk, tq), F32)], axis=0)
        qa_scr[hh] = (qp * (HEAD_DIM ** -0.5) + jnp.transpose(bias_rows)).astype(BF16)

    own_off = (j & 1) * tq
    row = lax.broadcasted_iota(jnp.int32, (tq, tk), 0)
    col = lax.broadcasted_iota(jnp.int32, (tq, tk), 1)
    hidden = col - own_off > row
    st_own = pl.multiple_of(own_pair * tk, tk)
    for hh in range(hps):
        s = _dot_nt(qa_scr[hh], k_ref[0, hh, pl.ds(st_own, tk), :])
        s = jnp.where(hidden, NEG, s)
        s_scr[hh, own_pair] = s
        m_scr[hh] = lane_max(s)

    def pass1(ip, carry):
        st = pl.multiple_of(ip * tk, tk)
        for hh in range(hps):
            s = _dot_nt(qa_scr[hh], k_ref[0, hh, pl.ds(st, tk), :])
            s_scr[hh, ip] = s
            m_scr[hh] = jnp.maximum(m_scr[hh], lane_max(s))
        return carry

    lax.fori_loop(0, own_pair, pass1, 0)
    for hh in range(hps):
        m = jnp.max(m_scr[hh], axis=-1, keepdims=True)
        m_scr[hh] = jnp.broadcast_to(m, (tq, LANES))
        acc_scr[hh] = jnp.zeros((tq, LANES), F32)

    def pass2(ip, carry):
        st = pl.multiple_of(ip * tk, tk)
        for hh in range(hps):
            m = m_scr[hh]
            p = jnp.concatenate(
                [jnp.exp(s_scr[hh, ip, :, c * LANES:(c + 1) * LANES] - m).astype(BF16) for c in range(tk // LANES)],
                axis=-1)
            acc_scr[hh] += _dot(p, v_ref[0, hh, pl.ds(st, tk), :])
        return carry

    lax.fori_loop(0, own_pair + 1, pass2, 0)
    outs = []
    for hh in range(hps):
        acc = acc_scr[hh]
        denom = jnp.sum(jnp.where(lane == HEAD_DIM, acc, 0.0), axis=-1, keepdims=True)
        outs.append(acc / denom)
    slabs = [jnp.where(lane < HEAD_DIM, outs[2 * i], pltpu.roll(outs[2 * i + 1], HEAD_DIM, 1))
             for i in range(hps // 2)]
    o_ref[0] = jnp.concatenate(slabs, axis=-1).astype(o_ref.dtype)


def _moba_prompt(q_pad, k_aug, v_aug, kmean_pad, *, hps=4):
    b, h, t, _ = q_pad.shape
    n_blk = t // MOBA_BLOCK
    assert HEAD_DIM + n_blk <= LANES and n_blk % 2 == 0 and h % hps == 0 and hps % 2 == 0
    tq = MOBA_BLOCK
    return pl.pallas_call(
        functools.partial(_moba_prompt_kernel, n_blk=n_blk, hps=hps),
        grid=(b, h // hps, n_blk),
        in_specs=[pl.BlockSpec((1, hps, tq, LANES), lambda bi, hg, j: (bi, hg, j, 0)),
                  pl.BlockSpec((1, hps, t, LANES), lambda bi, hg, j: (bi, hg, 0, 0)),
                  pl.BlockSpec((1, hps, t, LANES), lambda bi, hg, j: (bi, hg, 0, 0)),
                  pl.BlockSpec((1, hps, n_blk, LANES), lambda bi, hg, j: (bi, hg, 0, 0))],
        out_specs=pl.BlockSpec((1, tq, hps * HEAD_DIM), lambda bi, hg, j: (bi, j, hg)),
        out_shape=jax.ShapeDtypeStruct((b, t, h * HEAD_DIM), BF16),
        scratch_shapes=[pltpu.VMEM((hps, tq, LANES), BF16),
                        pltpu.VMEM((hps, n_blk // 2, tq, 2 * tq), F32),
                        pltpu.VMEM((hps, tq, LANES), F32),
                        pltpu.VMEM((hps, tq, LANES), F32)],
        compiler_params=_cparams(("parallel", "parallel", "arbitrary")),
        name="moba_prompt_attention",
    )(q_pad, k_aug, v_aug, kmean_pad)


def _moba_sample_kernel(pt_ref, q_ref, knew_ref, vnew_ref, *refs, n_pages, n_q):
    del pt_ref
    k_refs = refs[:n_pages]
    v_refs = refs[n_pages:2 * n_pages]
    o_ref = refs[2 * n_pages]
    s_scr = refs[2 * n_pages + 1]
    n_row = n_q * N_HEADS
    pages_per_blk = MOBA_BLOCK // PAGE_SIZE
    n_blk = n_pages // pages_per_blk
    h_shift = N_HEADS.bit_length() - 1
    d_shift = HEAD_DIM.bit_length() - 1
    scale = HEAD_DIM ** -0.5

    q = q_ref[0]
    q_rep = jnp.concatenate([jnp.broadcast_to(q[i:i + 1, :], (N_HEADS, D_ATTN)) for i in range(n_q)], axis=0)
    r_i = lax.broadcasted_iota(jnp.int32, (n_row, D_ATTN), 0)
    c_i = lax.broadcasted_iota(jnp.int32, (n_row, D_ATTN), 1)
    own_head = (r_i & (N_HEADS - 1)) == (c_i >> d_shift)
    q_bd = jnp.where(own_head, q_rep, 0.0)
    q_hi, q_lo = _split_bf16(q_bd)

    lane = lax.broadcasted_iota(jnp.int32, (n_row, PAGE_SIZE), 1)
    gate = jnp.zeros((n_row, PAGE_SIZE), F32)
    for pg in range(n_pages):
        k_hi, k_lo = _split_bf16(k_refs[pg][...])
        s = _dot(q_hi, k_hi) + (_dot(q_lo, k_hi) + _dot(q_hi, k_lo))
        s_scr[pg] = s
        tot = jnp.sum(s, axis=-1, keepdims=True)
        gate = gate + jnp.where(lane == pg // pages_per_blk, tot, 0.0)
    sel = _top_select(gate * (1.0 / MOBA_BLOCK), lane < n_blk, lane, min(MOBA_TOPK, n_blk + 1), 1, PAGE_SIZE)
    sel_f = jnp.where(sel, 1.0, 0.0)

    knew = knew_ref[0]
    vnew = vnew_ref[0]
    q_idx = lax.broadcasted_iota(jnp.int32, (n_row, 1), 0) >> h_shift
    s_new = [jnp.where(q_idx >= jj, jnp.sum(q_bd * knew[jj:jj + 1, :], axis=-1, keepdims=True) * scale, NEG)
             for jj in range(n_q)]
    m = s_new[0]
    for jj in range(1, n_q):
        m = jnp.maximum(m, s_new[jj])
    chosen = [jnp.max(jnp.where(lane == bi, sel_f, 0.0), axis=-1, keepdims=True) > 0.0 for bi in range(n_blk)]
    mpart = jnp.full((n_row, PAGE_SIZE), NEG, F32)
    for pg in range(n_pages):
        s = jnp.where(chosen[pg // pages_per_blk], s_scr[pg] * scale, NEG)
        s_scr[pg] = s
        mpart = jnp.maximum(mpart, s)
    m = jnp.maximum(m, jnp.max(mpart, axis=-1, keepdims=True))

    acc = jnp.zeros((n_row, D_ATTN), F32)
    l = jnp.zeros((n_row, 1), F32)
    for jj in range(n_q):
        p = jnp.exp(s_new[jj] - m)
        l = l + p
        acc = acc + p * vnew[jj:jj + 1, :]
    lpart = jnp.zeros((n_row, PAGE_SIZE), F32)
    for pg in range(n_pages):
        p = jnp.exp(s_scr[pg] - m)
        lpart = lpart + p
        acc = acc + _dot_nt(p.astype(BF16), v_refs[pg][...].astype(BF16))
    l = l + jnp.sum(lpart, axis=-1, keepdims=True)
    out = jnp.where(own_head, acc / l, 0.0).reshape(n_q, N_HEADS, D_ATTN)
    o_ref[0] = jnp.sum(out, axis=1)


def _moba_sample(page_table, q, k_new, v_new, cache_kt, cache_vt, layer):
    n_seq, n_pages = page_table.shape
    n_q = q.shape[1]
    seq3 = pl.BlockSpec((1, n_q, D_ATTN), lambda b, pt: (b, 0, 0))

    def page_spec(pg):
        return pl.BlockSpec((None, None, D_ATTN, PAGE_SIZE), lambda b, pt, pg=pg: (layer, pt[b, pg], 0, 0))

    in_specs = [seq3, seq3, seq3] + [page_spec(pg) for pg in range(n_pages)] * 2
    grid_spec = pltpu.PrefetchScalarGridSpec(
        num_scalar_prefetch=1, grid=(n_seq,), in_specs=in_specs, out_specs=seq3,
        scratch_shapes=[pltpu.VMEM((n_pages, n_q * N_HEADS, PAGE_SIZE), F32)])
    return pl.pallas_call(
        functools.partial(_moba_sample_kernel, n_pages=n_pages, n_q=n_q),
        grid_spec=grid_spec,
        out_shape=jax.ShapeDtypeStruct((n_seq, n_q, D_ATTN), F32),
        compiler_params=_cparams(("arbitrary",)),
        name="moba_sample_attention",
    )(page_table, q, k_new, v_new, *([cache_kt] * n_pages), *([cache_vt] * n_pages))


def _conv_sample_kernel(st_ref, u_ref, wst_ref, wu_ref, b_ref, o_ref, *, n_q):
    st = st_ref[...]
    u = u_ref[...]
    for t in range(n_q):
        y = jnp.sum(st * wst_ref[t], axis=1) + jnp.sum(u * wu_ref[t], axis=1)
        o_ref[t] = y + b_ref[...]


def _conv_sample(state, u, w_dw, b_dw):
    n_seq, ctx, c = state.shape
    n_q = u.shape[1]
    r = jnp.arange(ctx)[None, :] - jnp.arange(n_q)[:, None]
    w_state = jnp.where((r >= 0)[..., None], w_dw[jnp.clip(r, 0, CONV_W - 1)], 0.0)
    ju = ctx + jnp.arange(n_q)[None, :] - jnp.arange(n_q)[:, None]
    w_new = jnp.where((ju <= ctx)[..., None], w_dw[jnp.clip(ju, 0, CONV_W - 1)], 0.0)
    sb = 16
    out = pl.pallas_call(
        functools.partial(_conv_sample_kernel, n_q=n_q),
        grid=(n_seq // sb,),
        in_specs=[pl.BlockSpec((sb, ctx, c), lambda i: (i, 0, 0)),
                  pl.BlockSpec((sb, n_q, c), lambda i: (i, 0, 0)),
                  pl.BlockSpec((n_q, ctx, c), lambda i: (0, 0, 0)),
                  pl.BlockSpec((n_q, n_q, c), lambda i: (0, 0, 0)),
                  pl.BlockSpec((1, c), lambda i: (0, 0))],
        out_specs=pl.BlockSpec((n_q, sb, c), lambda i: (0, i, 0)),
        out_shape=jax.ShapeDtypeStruct((n_q, n_seq, c), F32),
        compiler_params=_cparams(("parallel",)),
        name="conv_sample",
    )(state, u, w_state, w_new, b_dw.reshape(1, c))
    return out.transpose(1, 0, 2)


def _post_kernel(*refs, tm, do_conv, do_route):
    it = iter(refs)
    x_ref, u_ref = next(it), next(it)
    halo_ref = next(it) if do_conv else None
    attn_ref, ga_ref, gc_ref, gm_ref, sf_ref, scf_ref = (next(it) for _ in range(6))
    if do_conv:
        wdw_ref, bdw_ref = next(it), next(it)
    lng_ref, lnb_ref, wpw_ref, wo_ref, wout_ref, g2_ref = (next(it) for _ in range(6))
    if do_route:
        rwh_ref, rwl_ref, rb_ref = next(it), next(it), next(it)
    xo_ref, h2_ref = next(it), next(it)
    gate_ref = next(it) if do_route else None
    if do_conv:
        xbuf, ybuf = next(it), next(it)

    if do_conv:
        i = pl.program_id(1)
        xbuf[0:CONV_HALO, :] = jnp.where(i > 0, halo_ref[0], 0.0)
        xbuf[CONV_HALO:CONV_HALO + tm, :] = u_ref[0]
        off = CONV_HALO - (CONV_W - 1)
        for c0 in range(0, D_CONV, LANES):
            acc = jnp.broadcast_to(bdw_ref[:, c0:c0 + LANES], (tm, LANES))
            for w in range(CONV_W):
                acc = acc + xbuf[off + w:off + w + tm, c0:c0 + LANES] * wdw_ref[w:w + 1, c0:c0 + LANES]
            ybuf[:, c0:c0 + LANES] = acc
        yc = ybuf[...]
    else:
        yc = u_ref[0]

    mu = jnp.mean(yc, axis=-1, keepdims=True)
    var = jnp.mean(jnp.square(yc - mu), axis=-1, keepdims=True)
    yn = (yc - mu) * lax.rsqrt(var + LN_EPS) * lng_ref[...] + lnb_ref[...]
    y_conv = _dot((yn * _sigmoid(yn)).astype(BF16), wpw_ref[...])
    y_attn = _dot(attn_ref[0], wo_ref[...])
    merged = ga_ref[0].astype(F32) * y_attn + gc_ref[0].astype(F32) * y_conv
    xn = x_ref[0] + gm_ref[0] * _dot(merged.astype(BF16), wout_ref[...])
    xo_ref[0] = xn
    ms = jnp.mean(xn * xn, axis=-1, keepdims=True)
    h2 = (xn * lax.rsqrt(ms + NORM_EPS)) * g2_ref[...]
    h2 = h2 * (1.0 + scf_ref[0]) + sf_ref[0]
    h2_ref[0] = h2.astype(BF16)
    if do_route:
        h_hi, h_lo = _split_bf16(h2)
        logits = (_dot(h_hi, rwh_ref[...]) + (_dot(h_lo, rwh_ref[...]) + _dot(h_hi, rwl_ref[...]))) + rb_ref[...]
        eidx = lax.broadcasted_iota(jnp.int32, logits.shape, 1)
        top1 = jnp.max(logits, axis=-1, keepdims=True)
        i1 = jnp.min(jnp.where(logits == top1, eidx, N_EXPERTS), axis=-1, keepdims=True)
        rest = jnp.where(eidx == i1, -jnp.inf, logits)
        top2 = jnp.max(rest, axis=-1, keepdims=True)
        i2 = jnp.min(jnp.where(rest == top2, eidx, N_EXPERTS), axis=-1, keepdims=True)
        e2 = jnp.exp(top2 - top1)
        w1 = 1.0 / (1.0 + e2)
        w2 = e2 / (1.0 + e2)
        gate_ref[0] = jnp.where(eidx == i1, w1, 0.0) + jnp.where(eidx == i2, w2, 0.0)


def _post_mixer(x, u, attn, ga, gc, gate_m, shift_f, scale_f, conv_w, ln_g, ln_b, w_pw_b, w_o_b, w_out_b, g2,
                router, *, tm):
    g, tg, d = x.shape
    r = gate_m.shape[1]
    do_conv = conv_w is not None
    do_route = router is not None
    if r == 1:
        mod_spec = pl.BlockSpec((1, 1, d), lambda b, i: (b, 0, 0))
    else:
        mod_spec = pl.BlockSpec((1, tm, d), lambda b, i: (b, i, 0))
    tok = lambda w: pl.BlockSpec((1, tm, w), lambda b, i: (b, i, 0))
    const2 = lambda b, i: (0, 0)
    args = [x, u]
    specs = [tok(d), tok(D_CONV)]
    if do_conv:
        per = tm // CONV_HALO
        args.append(u)
        specs.append(pl.BlockSpec((1, CONV_HALO, D_CONV), lambda b, i: (b, jnp.maximum(i * per - 1, 0), 0)))
    args += [attn, ga, gc, gate_m, shift_f, scale_f]
    specs += [tok(D_ATTN), tok(d), tok(d), mod_spec, mod_spec, mod_spec]
    if do_conv:
        args += [conv_w[0], conv_w[1]]
        specs += [pl.BlockSpec((CONV_W, D_CONV), const2), pl.BlockSpec((1, D_CONV), const2)]
    args += [ln_g, ln_b, w_pw_b, w_o_b, w_out_b, g2]
    specs += [pl.BlockSpec((1, D_CONV), const2), pl.BlockSpec((1, D_CONV), const2),
              pl.BlockSpec((D_CONV, d), const2), pl.BlockSpec((D_ATTN, d), const2),
              pl.BlockSpec((d, d), const2), pl.BlockSpec((1, d), const2)]
    if do_route:
        args += list(router)
        specs += [pl.BlockSpec((d, N_EXPERTS), const2), pl.BlockSpec((d, N_EXPERTS), const2),
                  pl.BlockSpec((1, N_EXPERTS), const2)]
    out_shape = [jax.ShapeDtypeStruct((g, tg, d), F32), jax.ShapeDtypeStruct((g, tg, d), BF16)]
    out_specs = [tok(d), tok(d)]
    if do_route:
        out_shape.append(jax.ShapeDtypeStruct((g, tg, N_EXPERTS), F32))
        out_specs.append(tok(N_EXPERTS))
    scratch = []
    if do_conv:
        scratch = [pltpu.VMEM((CONV_HALO + tm, D_CONV), F32), pltpu.VMEM((tm, D_CONV), F32)]
    outs = pl.pallas_call(
        functools.partial(_post_kernel, tm=tm, do_conv=do_conv, do_route=do_route),
        grid=(g, tg // tm),
        in_specs=specs,
        out_specs=tuple(out_specs),
        out_shape=tuple(out_shape),
        scratch_shapes=scratch,
        compiler_params=_cparams(("parallel", "arbitrary")),
        name="post_mixer",
    )(*args)
    return outs


def _ffn_kernel(h_ref, x_ref, gf_ref, w1_ref, w3_ref, w2_ref, o_ref, acc_ref):
    f = pl.program_id(2)

    @pl.when(f == 0)
    def _():
        acc_ref[...] = jnp.zeros_like(acc_ref)

    hb = h_ref[0]
    a1 = _dot(hb, w1_ref[...])
    a3 = _dot(hb, w3_ref[...])
    act = (a1 * _sigmoid(a1) * a3).astype(BF16)
    acc_ref[...] += _dot(act, w2_ref[...])

    @pl.when(f == pl.num_programs(2) - 1)
    def _():
        o_ref[0] = x_ref[0] + gf_ref[0] * acc_ref[...]


def _ffn_dense(h2, x, gate_f, w1b, w3b, w2b, *, tm, tf):
    g, tg, d = x.shape
    dff = w1b.shape[1]
    tm = min(tm, tg)
    r = gate_f.shape[1]
    if r == 1:
        mod_spec = pl.BlockSpec((1, 1, d), lambda b, i, f: (b, 0, 0))
    else:
        mod_spec = pl.BlockSpec((1, tm, d), lambda b, i, f: (b, i, 0))
    tok = pl.BlockSpec((1, tm, d), lambda b, i, f: (b, i, 0))
    return pl.pallas_call(
        _ffn_kernel,
        grid=(g, tg // tm, dff // tf),
        in_specs=[tok, tok, mod_spec,
                  pl.BlockSpec((d, tf), lambda b, i, f: (0, f)),
                  pl.BlockSpec((d, tf), lambda b, i, f: (0, f)),
                  pl.BlockSpec((tf, d), lambda b, i, f: (f, 0))],
        out_specs=tok,
        out_shape=jax.ShapeDtypeStruct((g, tg, d), F32),
        scratch_shapes=[pltpu.VMEM((tm, d), F32)],
        compiler_params=_cparams(("parallel", "parallel", "arbitrary")),
        name="ffn_dense",
    )(h2, x, gate_f, w1b, w3b, w2b)


MOE_TILE = 256
MOE_ALIGN = 16


def _moe_dispatch_kernel(a_ref, h_ref, rlt_ref, xs_in_ref, xs_ref, cbuf, sem, *, tm, n_e):
    del xs_in_ref
    i = pl.program_id(0)
    n = pl.num_programs(0)
    slot = i % 2
    hb = h_ref[...]
    kio = lax.broadcasted_iota(jnp.int32, (tm, tm), 0)
    for e in range(n_e):
        pt = jnp.where(kio == rlt_ref[e:e + 1, :], 1.0, 0.0).astype(BF16)
        cbuf[slot, e] = _dot(pt, hb).astype(BF16)

    def copies(step, sl):
        return [pltpu.make_async_copy(
            cbuf.at[sl, e], xs_ref.at[pl.ds(pl.multiple_of(a_ref[step * n_e + e], MOE_ALIGN), tm)], sem.at[sl])
            for e in range(n_e)]

    @pl.when(i > 0)
    def _():
        for cp in copies(i - 1, 1 - slot):
            cp.wait()

    for cp in copies(i, slot):
        cp.start()

    @pl.when(i == n - 1)
    def _():
        for cp in copies(i, slot):
            cp.wait()


def _moe_expert_kernel(exp_ref, act_ref, x_ref, w1_ref, w3_ref, w2_ref, yh_ref, yl_ref, acc_ref, *, tf):
    del exp_ref
    i = pl.program_id(0)

    @pl.when(act_ref[i] > 0)
    def _():
        xb = x_ref[...]
        dff = w1_ref.shape[1]
        for f0 in range(0, dff, tf):
            a1 = _dot(xb, w1_ref[:, f0:f0 + tf])
            a3 = _dot(xb, w3_ref[:, f0:f0 + tf])
            part = _dot((a1 * _sigmoid(a1) * a3).astype(BF16), w2_ref[f0:f0 + tf, :])
            if f0 == 0:
                acc_ref[...] = part
            else:
                acc_ref[...] += part
        y_hi, y_lo = _split_bf16(acc_ref[...])
        yh_ref[...] = y_hi
        yl_ref[...] = y_lo

    @pl.when(act_ref[i] == 0)
    def _():
        yh_ref[...] = jnp.zeros(yh_ref.shape, BF16)
        yl_ref[...] = jnp.zeros(yl_ref.shape, BF16)


def _moe_combine_kernel(a_ref, yh_hbm, yl_hbm, x_ref, gf_ref, g_ref, rl_ref, o_ref, ybuf, sem, *, tm, n_e):
    i = pl.program_id(0)
    n = pl.num_programs(0)
    slot = i % 2

    def copies(step, sl):
        out = []
        for e in range(n_e):
            rows = pl.ds(pl.multiple_of(a_ref[step * n_e + e], MOE_ALIGN), tm)
            out.append(pltpu.make_async_copy(yh_hbm.at[rows], ybuf.at[sl, 0, e], sem.at[sl]))
            out.append(pltpu.make_async_copy(yl_hbm.at[rows], ybuf.at[sl, 1, e], sem.at[sl]))
        return out

    @pl.when(i == 0)
    def _():
        for cp in copies(0, 0):
            cp.start()

    @pl.when(i + 1 < n)
    def _():
        for cp in copies(i + 1, 1 - slot):
            cp.start()

    for cp in copies(i, slot):
        cp.wait()

    lio = lax.broadcasted_iota(jnp.int32, (tm, tm), 1)
    acc = jnp.zeros(o_ref.shape, F32)
    for e in range(n_e):
        p = jnp.where(lio == rl_ref[:, e:e + 1], 1.0, 0.0).astype(BF16)
        z = _dot(p, ybuf[slot, 0, e]) + _dot(p, ybuf[slot, 1, e])
        acc = acc + g_ref[:, e:e + 1] * z
    o_ref[...] = x_ref[...] + gf_ref[...] * acc


def _ffn_moe_routed(h2, x, gate_f, gates, w1b, w3b, w2b, *, tf):
    t, d = x.shape
    n_e, _, dff = w1b.shape
    tile = MOE_TILE
    tm = next(c for c in (256, 128, 64, 32, 16) if t % c == 0)
    nt = t // tm
    ceil_to = lambda v, m: ((v + m - 1) // m) * m
    n_tiles = (2 * t + nt * n_e * (MOE_ALIGN - 1)) // tile + 2 * n_e + 1
    s_pad = n_tiles * tile

    mask = gates > 0.0
    m3 = mask.reshape(nt, tm, n_e).astype(jnp.int32)
    rl = jnp.where(mask, (jnp.cumsum(m3, axis=1) - 1).reshape(t, n_e), -1)
    p_te = ceil_to(jnp.sum(m3, axis=1), MOE_ALIGN)
    act_rows = ceil_to(jnp.sum(p_te, axis=0), tile)
    seg = act_rows + tile
    seg_start = jnp.cumsum(seg) - seg
    a_te = (seg_start[None, :] + jnp.cumsum(p_te, axis=0) - p_te).astype(jnp.int32).reshape(nt * n_e)
    tile_start = jnp.arange(n_tiles, dtype=jnp.int32) * tile
    tile_exp = jnp.minimum(jnp.sum((tile_start[:, None] >= (seg_start + seg)[None, :]).astype(jnp.int32), axis=1),
                           n_e - 1)
    in_seg = tile_start - seg_start[tile_exp]
    tile_act = ((in_seg >= 0) & (in_seg < act_rows[tile_exp])).astype(jnp.int32)

    xs = pl.pallas_call(
        functools.partial(_moe_dispatch_kernel, tm=tm, n_e=n_e),
        grid_spec=pltpu.PrefetchScalarGridSpec(
            num_scalar_prefetch=1, grid=(nt,),
            in_specs=[pl.BlockSpec((tm, d), lambda i, a: (i, 0)),
                      pl.BlockSpec((n_e, tm), lambda i, a: (0, i)),
                      pl.BlockSpec(memory_space=pl.ANY)],
            out_specs=pl.BlockSpec(memory_space=pl.ANY),
            scratch_shapes=[pltpu.VMEM((2, n_e, tm, d), BF16), pltpu.SemaphoreType.DMA((2,))]),
        out_shape=jax.ShapeDtypeStruct((s_pad, d), BF16),
        input_output_aliases={3: 0},
        compiler_params=_cparams(("arbitrary",)),
        name="moe_dispatch",
    )(a_te, h2, rl.T, jnp.zeros((s_pad, d), BF16))

    wspec = lambda shp: pl.BlockSpec((None,) + shp, lambda i, ex, ac: (ex[i], 0, 0))
    slot_spec = pl.BlockSpec((tile, d), lambda i, ex, ac: (i, 0))
    y_hi, y_lo = pl.pallas_call(
        functools.partial(_moe_expert_kernel, tf=tf),
        grid_spec=pltpu.PrefetchScalarGridSpec(
            num_scalar_prefetch=2, grid=(n_tiles,),
            in_specs=[slot_spec, wspec((d, dff)), wspec((d, dff)), wspec((dff, d))],
            out_specs=(slot_spec, slot_spec),
            scratch_shapes=[pltpu.VMEM((tile, d), F32)]),
        out_shape=(jax.ShapeDtypeStruct((s_pad, d), BF16), jax.ShapeDtypeStruct((s_pad, d), BF16)),
        compiler_params=_cparams(("arbitrary",)),
        name="moe_experts",
    )(tile_exp, tile_act, xs, w1b, w3b, w2b)

    tok = lambda w: pl.BlockSpec((tm, w), lambda i, a: (i, 0))
    return pl.pallas_call(
        functools.partial(_moe_combine_kernel, tm=tm, n_e=n_e),
        grid_spec=pltpu.PrefetchScalarGridSpec(
            num_scalar_prefetch=1, grid=(nt,),
            in_specs=[pl.BlockSpec(memory_space=pl.ANY), pl.BlockSpec(memory_space=pl.ANY),
                      tok(d), tok(d), tok(n_e), tok(n_e)],
            out_specs=tok(d),
            scratch_shapes=[pltpu.VMEM((2, 2, n_e, tm, d), BF16), pltpu.SemaphoreType.DMA((2,))]),
        out_shape=jax.ShapeDtypeStruct((t, d), F32),
        compiler_params=_cparams(("arbitrary",)),
        name="moe_combine",
    )(a_te, y_hi, y_lo, x, gate_f, gates, rl)


def kernel(x_prompt, x_sample, cache_k, cache_v, state_conv, page_table, c_prompt, c_sample, w_mod, b_mod,
           norm1_g, norm2_g, w_in, q_norm_g, k_norm_g, w_o_attn, w_dw, b_dw, conv_ln_g, conv_ln_b, w_pw_conv,
           w_out, ffn_w1, ffn_w3, ffn_w2, router_w, router_b, moe_w1, moe_w3, moe_w2):
    depth = w_in.shape[0]
    n_p, t_p, d = x_prompt.shape
    n_s, t_s, _ = x_sample.shape
    n_tok_s = n_s * t_s
    n_phys = cache_k.shape[1]
    n_blk_p = t_p // MOBA_BLOCK

    c_all = jnp.concatenate([c_prompt, c_sample], axis=0)
    c_all = jnp.pad(c_all, ((0, (-c_all.shape[0]) % 8), (0, 0)))
    mod = _modulation(c_all, w_mod, b_mod)

    cache_kt = cache_k.transpose(0, 1, 3, 4, 2).reshape(depth, n_phys, D_ATTN, PAGE_SIZE)
    cache_vt = cache_v.transpose(0, 1, 3, 4, 2).reshape(depth, n_phys, D_ATTN, PAGE_SIZE)

    head_id = jnp.arange(D_ATTN) // HEAD_DIM
    head_mean = jnp.where(head_id[:, None] == head_id[None, :], 1.0 / HEAD_DIM, 0.0).astype(BF16)

    xp = x_prompt
    xs = x_sample.reshape(1, n_tok_s, d)
    k_p, v_p, conv_p, k_s, v_s, conv_s = [], [], [], [], [], []
    for l in range(depth):
        mod_p = mod[l, :n_p].reshape(n_p, N_MOD, 1, d)
        mod_s = jnp.repeat(mod[l, n_p:n_p + n_s].reshape(n_s, N_MOD, d), t_s, axis=0)
        mp = [mod_p[:, i] for i in range(N_MOD)]
        msm = [mod_s[None, :, i] for i in range(N_MOD)]

        w_in_b = w_in[l].astype(BF16)
        g1 = norm1_g[l].reshape(1, d)
        g2 = norm2_g[l].reshape(1, d)
        gq = jnp.tile(q_norm_g[l], N_HEADS).reshape(1, D_ATTN)
        gk = jnp.tile(k_norm_g[l], N_HEADS).reshape(1, D_ATTN)
        w_o_b = w_o_attn[l].astype(BF16)
        w_pw_b = w_pw_conv[l].astype(BF16)
        w_out_b = w_out[l].astype(BF16)
        ln_g = conv_ln_g[l].reshape(1, D_CONV)
        ln_b = conv_ln_b[l].reshape(1, D_CONV)
        is_moe = l % 2 == 1
        router = None
        if is_moe:
            rw = router_w[l // 2]
            rw_hi = rw.astype(BF16)
            rw_lo = (rw - rw_hi.astype(F32)).astype(BF16)
            router = (rw_hi, rw_lo, router_b[l // 2].reshape(1, N_EXPERTS))

        q_pad, k_aug, v_aug, kf, vf, u, ga, gc, km = _in_projection(
            xp, mp[0], mp[1], g1, w_in_b, head_mean, gq, gk, tm=MOBA_BLOCK, head_major=True)
        kmean = km.reshape(n_p, n_blk_p, N_HEADS, HEAD_DIM).transpose(0, 2, 1, 3)
        kmean = jnp.pad(kmean, ((0, 0), (0, 0), (0, 0), (0, LANES - HEAD_DIM)))
        attn = _moba_prompt(q_pad, k_aug, v_aug, kmean)
        outs = _post_mixer(xp, u, attn, ga, gc, mp[2], mp[3], mp[4], (w_dw[l], b_dw[l].reshape(1, D_CONV)),
                           ln_g, ln_b, w_pw_b, w_o_b, w_out_b, g2, router, tm=256)
        k_p.append(kf.reshape(n_p, t_p, N_HEADS, HEAD_DIM))
        v_p.append(vf.reshape(n_p, t_p, N_HEADS, HEAD_DIM))
        conv_p.append(u[:, t_p - (CONV_W - 1):])
        xp_mid, h2_p = outs[0], outs[1]
        gates_p = outs[2] if is_moe else None

        qs, kfs, vfs, us, gas, gcs = _in_projection(
            xs, msm[0], msm[1], g1, w_in_b, head_mean, gq, gk, tm=min(256, n_tok_s), head_major=False)
        seq = lambda a: a.reshape(n_s, t_s, a.shape[-1])
        attn_s = _moba_sample(page_table, seq(qs), seq(kfs), seq(vfs), cache_kt, cache_vt, l)
        attn_s = attn_s.reshape(1, n_tok_s, D_ATTN).astype(BF16)
        u_s = seq(us)
        y_dw = _conv_sample(state_conv[l], u_s, w_dw[l], b_dw[l]).reshape(1, n_tok_s, D_CONV)
        outs = _post_mixer(xs, y_dw, attn_s, gas, gcs, msm[2], msm[3], msm[4], None,
                           ln_g, ln_b, w_pw_b, w_o_b, w_out_b, g2, router, tm=min(256, n_tok_s))
        k_s.append(kfs.reshape(n_s, t_s, N_HEADS, HEAD_DIM))
        v_s.append(vfs.reshape(n_s, t_s, N_HEADS, HEAD_DIM))
        conv_s.append(jnp.concatenate([state_conv[l], u_s], axis=1)[:, -(CONV_W - 1):])
        xs_mid, h2_s = outs[0], outs[1]
        gates_s = outs[2] if is_moe else None

        if is_moe:
            w1b = moe_w1[l // 2].astype(BF16)
            w3b = moe_w3[l // 2].astype(BF16)
            w2b = moe_w2[l // 2].astype(BF16)
            n_tok_p = n_p * t_p
            flat = lambda a_p, a_s: jnp.concatenate([a_p.reshape(n_tok_p, -1), a_s.reshape(n_tok_s, -1)], axis=0)
            gf_tok = flat(jnp.broadcast_to(mp[5], (n_p, t_p, d)), msm[5])
            x_all = _ffn_moe_routed(flat(h2_p, h2_s), flat(xp_mid, xs_mid), gf_tok, flat(gates_p, gates_s),
                                    w1b, w3b, w2b, tf=1408)
            xp = x_all[:n_tok_p].reshape(n_p, t_p, d)
            xs = x_all[n_tok_p:].reshape(1, n_tok_s, d)
        else:
            w1b = ffn_w1[l // 2].astype(BF16)
            w3b = ffn_w3[l // 2].astype(BF16)
            w2b = ffn_w2[l // 2].astype(BF16)
            xp = _ffn_dense(h2_p, xp_mid, mp[5], w1b, w3b, w2b, tm=512, tf=1408)
            xs = _ffn_dense(h2_s, xs_mid, msm[5], w1b, w3b, w2b, tm=512, tf=1408)

    return (xp, xs.reshape(n_s, t_s, d), jnp.stack(k_p), jnp.stack(v_p), jnp.stack(conv_p),
            jnp.stack(k_s), jnp.stack(v_s), jnp.stack(conv_s))
```

```python
import functools

import jax
import jax.numpy as jnp
from jax import lax
from jax.experimental import pallas as pl
from jax.experimental.pallas import tpu as pltpu

F32 = jnp.float32
BF16 = jnp.bfloat16
HIGHEST = lax.Precision.HIGHEST

D_MODEL = 1024
N_HEADS = 8
HEAD_DIM = 64
D_ATTN = N_HEADS * HEAD_DIM
MOBA_BLOCK = 256
MOBA_TOPK = 3
D_CONV = 512
CONV_W = 31
N_EXPERTS = 8
N_MOD = 6
PAGE_SIZE = 128
NORM_EPS = 1e-6
LN_EPS = 1e-5
NEG = -1e30
LANES = 128
SUBLANES = 8
CONV_HALO = 32
VMEM_LIMIT = 56 * 1024 * 1024


def _cparams(sem):
    return pltpu.CompilerParams(dimension_semantics=sem, vmem_limit_bytes=VMEM_LIMIT)


def _dot(a, b):
    return jnp.dot(a, b, preferred_element_type=F32)


def _dot_nt(a, b, precision=None):
    return lax.dot_general(a, b, (((1,), (1,)), ((), ())), precision=precision, preferred_element_type=F32)


def _sigmoid(x):
    return 1.0 / (1.0 + jnp.exp(-x))


def _split_bf16(x):
    hi = x.astype(BF16)
    lo = (x - hi.astype(F32)).astype(BF16)
    return hi, lo


def _top_select(score, valid, idx, n_sel, axis, sentinel):
    remaining = valid
    sel = jnp.zeros(score.shape, dtype=jnp.bool_)
    for _ in range(n_sel):
        cur = jnp.where(remaining, score, -jnp.inf)
        best = jnp.max(cur, axis=axis, keepdims=True)
        cand = remaining & (cur == best)
        first = jnp.min(jnp.where(cand, idx, sentinel), axis=axis, keepdims=True)
        pick = idx == first
        sel = sel | pick
        remaining = remaining & jnp.logical_not(pick)
    return sel


def _mod_kernel(c_ref, w_ref, b_ref, o_ref):
    c = c_ref[...]
    cond = c * _sigmoid(c)
    o_ref[0] = _dot(cond.astype(BF16), w_ref[0].astype(BF16)) + b_ref[0]


def _modulation(c_all, w_mod, b_mod):
    depth, d, n = w_mod.shape
    rows = c_all.shape[0]
    tn = 1536
    return pl.pallas_call(
        _mod_kernel,
        grid=(depth, n // tn),
        in_specs=[
            pl.BlockSpec((rows, d), lambda l, j: (0, 0)),
            pl.BlockSpec((1, d, tn), lambda l, j: (l, 0, j)),
            pl.BlockSpec((1, 1, tn), lambda l, j: (l, 0, j)),
        ],
        out_specs=pl.BlockSpec((1, rows, tn), lambda l, j: (l, 0, j)),
        out_shape=jax.ShapeDtypeStruct((depth, rows, n), F32),
        compiler_params=_cparams(("parallel", "parallel")),
        name="adaln_modulation",
    )(c_all, w_mod, b_mod.reshape(depth, 1, n))


def _inproj_kernel(x_ref, shift_ref, scale_ref, g1_ref, w_ref, hm_ref, gq_ref, gk_ref, *out_refs, tm, head_major):
    if head_major:
        qp_ref, ka_ref, va_ref, kf_ref, vf_ref, u_ref, ga_ref, gc_ref, km_ref = out_refs
    else:
        qf_ref, kf_ref, vf_ref, u_ref, ga_ref, gc_ref = out_refs
    x = x_ref[0]
    ms = jnp.mean(x * x, axis=-1, keepdims=True)
    h = (x * lax.rsqrt(ms + NORM_EPS)) * g1_ref[...]
    h = h * (1.0 + scale_ref[0]) + shift_ref[0]
    hb = h.astype(BF16)

    def proj(lo, hi):
        return _dot(hb, w_ref[:, lo:hi])

    def head_rms(z, g_ref):
        z2_hi, z2_lo = _split_bf16(z * z)
        msh = _dot(z2_hi, hm_ref[...]) + _dot(z2_lo, hm_ref[...])
        return z * lax.rsqrt(msh + NORM_EPS) * g_ref[...]

    b0, b1, b2, b3, b4, b5, b6 = (0, D_ATTN, 2 * D_ATTN, 3 * D_ATTN, 3 * D_ATTN + D_CONV,
                                  3 * D_ATTN + 2 * D_CONV, 3 * D_ATTN + 2 * D_CONV + D_MODEL)
    q = head_rms(proj(b0, b1), gq_ref)
    k = head_rms(proj(b1, b2), gk_ref)
    v = proj(b2, b3)
    kf_ref[0] = k
    vf_ref[0] = v
    if head_major:
        blk = pl.program_id(1) * (tm // MOBA_BLOCK)
        lane = lax.broadcasted_iota(jnp.int32, (tm, LANES), 1)
        low = lane < HEAD_DIM
        k_tail = jnp.where(lane == HEAD_DIM + blk, 1.0, 0.0)
        v_tail = jnp.where(lane == HEAD_DIM, 1.0, 0.0)
        for hp in range(N_HEADS // 2):
            sl = slice(hp * LANES, (hp + 1) * LANES)
            for z, dst, tail in ((q, qp_ref, 0.0), (k, ka_ref, k_tail), (v, va_ref, v_tail)):
                pair = z[:, sl]
                dst[0, 2 * hp] = jnp.where(low, pair, tail).astype(dst.dtype)
                dst[0, 2 * hp + 1] = jnp.where(low, pltpu.roll(pair, HEAD_DIM, 1), tail).astype(dst.dtype)
        km_ref[0, 0] = jnp.mean(k, axis=0, keepdims=True)
    else:
        qf_ref[0] = q
    u_ref[0] = proj(b3, b4) * _sigmoid(proj(b4, b5))
    ga_ref[0] = _sigmoid(proj(b5, b6)).astype(BF16)
    gc_ref[0] = _sigmoid(proj(b6, b6 + D_MODEL)).astype(BF16)


def _in_projection(x, shift, scale, g1, w_in_b, head_mean, gq, gk, *, tm, head_major):
    g, tg, d = x.shape
    r = shift.shape[1]
    n_in = w_in_b.shape[1]
    if r == 1:
        mod_spec = pl.BlockSpec((1, 1, d), lambda b, i: (b, 0, 0))
    else:
        mod_spec = pl.BlockSpec((1, tm, d), lambda b, i: (b, i, 0))
    const2 = lambda b, i: (0, 0)
    tok = lambda w: pl.BlockSpec((1, tm, w), lambda b, i: (b, i, 0))
    tok_shape = lambda w, dt: jax.ShapeDtypeStruct((g, tg, w), dt)
    common_shapes = (tok_shape(D_ATTN, F32), tok_shape(D_ATTN, F32), tok_shape(D_CONV, F32),
                     tok_shape(d, BF16), tok_shape(d, BF16))
    common_specs = (tok(D_ATTN), tok(D_ATTN), tok(D_CONV), tok(d), tok(d))
    if head_major:
        assert tm == MOBA_BLOCK
        heads = pl.BlockSpec((1, N_HEADS, tm, LANES), lambda b, i: (b, 0, i, 0))
        hshape = lambda dt: jax.ShapeDtypeStruct((g, N_HEADS, tg, LANES), dt)
        out_shapes = (hshape(F32), hshape(BF16), hshape(BF16)) + common_shapes + (
            jax.ShapeDtypeStruct((g, tg // MOBA_BLOCK, 1, D_ATTN), F32),)
        out_specs = (heads, heads, heads) + common_specs + (
            pl.BlockSpec((1, 1, 1, D_ATTN), lambda b, i: (b, i, 0, 0)),)
    else:
        out_shapes = (tok_shape(D_ATTN, F32),) + common_shapes
        out_specs = (tok(D_ATTN),) + common_specs
    return pl.pallas_call(
        functools.partial(_inproj_kernel, tm=tm, head_major=head_major),
        grid=(g, tg // tm),
        in_specs=[tok(d), mod_spec, mod_spec,
                  pl.BlockSpec((1, d), const2),
                  pl.BlockSpec((d, n_in), const2),
                  pl.BlockSpec((D_ATTN, D_ATTN), const2),
                  pl.BlockSpec((1, D_ATTN), const2),
                  pl.BlockSpec((1, D_ATTN), const2)],
        out_specs=out_specs,
        out_shape=out_shapes,
        compiler_params=_cparams(("parallel", "parallel")),
        name="in_projection",
    )(x, shift, scale, g1, w_in_b, head_mean, gq, gk)


def _moba_prompt_kernel(q_ref, k_ref, v_ref, km_ref, o_ref, qa_scr, s_scr, m_scr, acc_scr, *, n_blk, hps):
    j = pl.program_id(2)
    tq = MOBA_BLOCK
    tk = 2 * MOBA_BLOCK
    own_pair = j // 2
    lane = lax.broadcasted_iota(jnp.int32, (tq, LANES), 1)
    blk_t = lax.broadcasted_iota(jnp.int32, (n_blk, tq), 0)

    def lane_max(s):
        m = s[:, :LANES]
        for c in range(1, tk // LANES):
            m = jnp.maximum(m, s[:, c * LANES:(c + 1) * LANES])
        return m

    for hh in range(hps):
        qp = q_ref[0, hh]
        gate_t = _dot_nt(km_ref[0, hh], qp, precision=HIGHEST)
        sel_t = _top_select(gate_t, blk_t < j, blk_t, min(MOBA_TOPK, n_blk), 0, n_blk)
        bias_t = jnp.where(sel_t | (blk_t == j), 0.0, NEG)
        bias_rows = jnp.concatenate([jnp.zeros((HEAD_DIM, tq), F32), bias_t,
                                     jnp.zeros((LANES - HEAD_DIM - n_blk, tq), F32)], axis=0)
        qa_scr[hh] = (qp * (HEAD_DIM ** -0.5) + jnp.transpose(bias_rows)).astype(BF16)

    own_off = (j & 1) * tq
    row = lax.broadcasted_iota(jnp.int32, (tq, tk), 0)
    col = lax.broadcasted_iota(jnp.int32, (tq, tk), 1)
    hidden = col - own_off > row
    st_own = pl.multiple_of(own_pair * tk, tk)
    for hh in range(hps):
        s = _dot_nt(qa_scr[hh], k_ref[0, hh, pl.ds(st_own, tk), :])
        s = jnp.where(hidden, NEG, s)
        s_scr[hh, own_pair] = s
        m_scr[hh] = lane_max(s)

    def pass1(ip, carry):
        st = pl.multiple_of(ip * tk, tk)
        for hh in range(hps):
            s = _dot_nt(qa_scr[hh], k_ref[0, hh, pl.ds(st, tk), :])
            s_scr[hh, ip] = s
            m_scr[hh] = jnp.maximum(m_scr[hh], lane_max(s))
        return carry

    lax.fori_loop(0, own_pair, pass1, 0)
    for hh in range(hps):
        m = jnp.max(m_scr[hh], axis=-1, keepdims=True)
        m_scr[hh] = jnp.broadcast_to(m, (tq, LANES))
        acc_scr[hh] = jnp.zeros((tq, LANES), F32)

    def pass2(ip, carry):
        st = pl.multiple_of(ip * tk, tk)
        for hh in range(hps):
            m = m_scr[hh]
            p = jnp.concatenate(
                [jnp.exp(s_scr[hh, ip, :, c * LANES:(c + 1) * LANES] - m).astype(BF16) for c in range(tk // LANES)],
                axis=-1)
            acc_scr[hh] += _dot(p, v_ref[0, hh, pl.ds(st, tk), :])
        return carry

    lax.fori_loop(0, own_pair + 1, pass2, 0)
    outs = []
    for hh in range(hps):
        acc = acc_scr[hh]
        denom = jnp.sum(jnp.where(lane == HEAD_DIM, acc, 0.0), axis=-1, keepdims=True)
        outs.append(acc / denom)
    slabs = [jnp.where(lane < HEAD_DIM, outs[2 * i], pltpu.roll(outs[2 * i + 1], HEAD_DIM, 1))
             for i in range(hps // 2)]
    o_ref[0] = jnp.concatenate(slabs, axis=-1).astype(o_ref.dtype)


def _moba_prompt(q_pad, k_aug, v_aug, kmean_pad, *, hps=4):
    b, h, t, _ = q_pad.shape
    n_blk = t // MOBA_BLOCK
    assert HEAD_DIM + n_blk <= LANES and n_blk % 2 == 0 and h % hps == 0 and hps % 2 == 0
    tq = MOBA_BLOCK
    return pl.pallas_call(
        functools.partial(_moba_prompt_kernel, n_blk=n_blk, hps=hps),
        grid=(b, h // hps, n_blk),
        in_specs=[pl.BlockSpec((1, hps, tq, LANES), lambda bi, hg, j: (bi, hg, j, 0)),
                  pl.BlockSpec((1, hps, t, LANES), lambda bi, hg, j: (bi, hg, 0, 0)),
                  pl.BlockSpec((1, hps, t, LANES), lambda bi, hg, j: (bi, hg, 0, 0)),
                  pl.BlockSpec((1, hps, n_blk, LANES), lambda bi, hg, j: (bi, hg, 0, 0))],
        out_specs=pl.BlockSpec((1, tq, hps * HEAD_DIM), lambda bi, hg, j: (bi, j, hg)),
        out_shape=jax.ShapeDtypeStruct((b, t, h * HEAD_DIM), BF16),
        scratch_shapes=[pltpu.VMEM((hps, tq, LANES), BF16),
                        pltpu.VMEM((hps, n_blk // 2, tq, 2 * tq), F32),
                        pltpu.VMEM((hps, tq, LANES), F32),
                        pltpu.VMEM((hps, tq, LANES), F32)],
        compiler_params=_cparams(("parallel", "parallel", "arbitrary")),
        name="moba_prompt_attention",
    )(q_pad, k_aug, v_aug, kmean_pad)


def _moba_sample_kernel(pt_ref, q_ref, knew_ref, vnew_ref, *refs, n_pages, n_q):
    del pt_ref
    k_refs = refs[:n_pages]
    v_refs = refs[n_pages:2 * n_pages]
    o_ref = refs[2 * n_pages]
    s_scr = refs[2 * n_pages + 1]
    n_row = n_q * N_HEADS
    pages_per_blk = MOBA_BLOCK // PAGE_SIZE
    n_blk = n_pages // pages_per_blk
    h_shift = N_HEADS.bit_length() - 1
    d_shift = HEAD_DIM.bit_length() - 1
    scale = HEAD_DIM ** -0.5

    q = q_ref[0]
    q_rep = jnp.concatenate([jnp.broadcast_to(q[i:i + 1, :], (N_HEADS, D_ATTN)) for i in range(n_q)], axis=0)
    r_i = lax.broadcasted_iota(jnp.int32, (n_row, D_ATTN), 0)
    c_i = lax.broadcasted_iota(jnp.int32, (n_row, D_ATTN), 1)
    own_head = (r_i & (N_HEADS - 1)) == (c_i >> d_shift)
    q_bd = jnp.where(own_head, q_rep, 0.0)
    q_hi, q_lo = _split_bf16(q_bd)

    lane = lax.broadcasted_iota(jnp.int32, (n_row, PAGE_SIZE), 1)
    q_both = jnp.concatenate([q_hi, q_lo], axis=0)
    block_of = lambda refs, bi: jnp.concatenate(
        [refs[bi * pages_per_blk + pp][...] for pp in range(pages_per_blk)], axis=1)
    gate = jnp.zeros((n_row, PAGE_SIZE), F32)
    for bi in range(n_blk):
        k_hi, k_lo = _split_bf16(block_of(k_refs, bi))
        both = _dot(q_both, k_hi)
        s = both[:n_row] + (both[n_row:] + _dot(q_hi, k_lo))
        s_scr[bi] = s
        tot = jnp.sum(s, axis=-1, keepdims=True)
        gate = gate + jnp.where(lane == bi, tot, 0.0)
    sel = _top_select(gate * (1.0 / MOBA_BLOCK), lane < n_blk, lane, min(MOBA_TOPK, n_blk + 1), 1, PAGE_SIZE)
    sel_f = jnp.where(sel, 1.0, 0.0)

    knew = knew_ref[0]
    vnew = vnew_ref[0]
    q_idx = lax.broadcasted_iota(jnp.int32, (n_row, 1), 0) >> h_shift
    s_new = [jnp.where(q_idx >= jj, jnp.sum(q_bd * knew[jj:jj + 1, :], axis=-1, keepdims=True) * scale, NEG)
             for jj in range(n_q)]
    m = s_new[0]
    for jj in range(1, n_q):
        m = jnp.maximum(m, s_new[jj])
    chosen = [jnp.max(jnp.where(lane == bi, sel_f, 0.0), axis=-1, keepdims=True) > 0.0 for bi in range(n_blk)]
    mpart = jnp.full((n_row, MOBA_BLOCK), NEG, F32)
    for bi in range(n_blk):
        s = jnp.where(chosen[bi], s_scr[bi] * scale, NEG)
        s_scr[bi] = s
        mpart = jnp.maximum(mpart, s)
    m = jnp.maximum(m, jnp.max(mpart, axis=-1, keepdims=True))

    acc = jnp.zeros((n_row, D_ATTN), F32)
    l = jnp.zeros((n_row, 1), F32)
    for jj in range(n_q):
        p = jnp.exp(s_new[jj] - m)
        l = l + p
        acc = acc + p * vnew[jj:jj + 1, :]
    lpart = jnp.zeros((n_row, MOBA_BLOCK), F32)
    for bi in range(n_blk):
        p = jnp.exp(s_scr[bi] - m)
        lpart = lpart + p
        acc = acc + _dot_nt(p.astype(BF16), block_of(v_refs, bi).astype(BF16))
    l = l + jnp.sum(lpart, axis=-1, keepdims=True)
    out = jnp.where(own_head, acc / l, 0.0).reshape(n_q, N_HEADS, D_ATTN)
    o_ref[0] = jnp.sum(out, axis=1)


def _moba_sample(page_table, q, k_new, v_new, cache_kt, cache_vt, layer):
    n_seq, n_pages = page_table.shape
    n_q = q.shape[1]
    seq3 = pl.BlockSpec((1, n_q, D_ATTN), lambda b, pt: (b, 0, 0))

    def page_spec(pg):
        return pl.BlockSpec((None, None, D_ATTN, PAGE_SIZE), lambda b, pt, pg=pg: (layer, pt[b, pg], 0, 0))

    in_specs = [seq3, seq3, seq3] + [page_spec(pg) for pg in range(n_pages)] * 2
    grid_spec = pltpu.PrefetchScalarGridSpec(
        num_scalar_prefetch=1, grid=(n_seq,), in_specs=in_specs, out_specs=seq3,
        scratch_shapes=[pltpu.VMEM((n_pages * PAGE_SIZE // MOBA_BLOCK, n_q * N_HEADS, MOBA_BLOCK), F32)])
    return pl.pallas_call(
        functools.partial(_moba_sample_kernel, n_pages=n_pages, n_q=n_q),
        grid_spec=grid_spec,
        out_shape=jax.ShapeDtypeStruct((n_seq, n_q, D_ATTN), F32),
        compiler_params=_cparams(("arbitrary",)),
        name="moba_sample_attention",
    )(page_table, q, k_new, v_new, *([cache_kt] * n_pages), *([cache_vt] * n_pages))


def _conv_sample_kernel(st_ref, u_ref, wst_ref, wu_ref, b_ref, o_ref, *, n_q):
    st = st_ref[...]
    u = u_ref[...]
    for t in range(n_q):
        y = jnp.sum(st * wst_ref[t], axis=1) + jnp.sum(u * wu_ref[t], axis=1)
        o_ref[t] = y + b_ref[...]


def _conv_sample(state, u, w_dw, b_dw):
    n_seq, ctx, c = state.shape
    n_q = u.shape[1]
    r = jnp.arange(ctx)[None, :] - jnp.arange(n_q)[:, None]
    w_state = jnp.where((r >= 0)[..., None], w_dw[jnp.clip(r, 0, CONV_W - 1)], 0.0)
    ju = ctx + jnp.arange(n_q)[None, :] - jnp.arange(n_q)[:, None]
    w_new = jnp.where((ju <= ctx)[..., None], w_dw[jnp.clip(ju, 0, CONV_W - 1)], 0.0)
    sb = 16
    out = pl.pallas_call(
        functools.partial(_conv_sample_kernel, n_q=n_q),
        grid=(n_seq // sb,),
        in_specs=[pl.BlockSpec((sb, ctx, c), lambda i: (i, 0, 0)),
                  pl.BlockSpec((sb, n_q, c), lambda i: (i, 0, 0)),
                  pl.BlockSpec((n_q, ctx, c), lambda i: (0, 0, 0)),
                  pl.BlockSpec((n_q, n_q, c), lambda i: (0, 0, 0)),
                  pl.BlockSpec((1, c), lambda i: (0, 0))],
        out_specs=pl.BlockSpec((n_q, sb, c), lambda i: (0, i, 0)),
        out_shape=jax.ShapeDtypeStruct((n_q, n_seq, c), F32),
        compiler_params=_cparams(("parallel",)),
        name="conv_sample",
    )(state, u, w_state, w_new, b_dw.reshape(1, c))
    return out.transpose(1, 0, 2)


def _post_kernel(*refs, tm, do_conv, do_route):
    it = iter(refs)
    x_ref, u_ref = next(it), next(it)
    halo_ref = next(it) if do_conv else None
    attn_ref, ga_ref, gc_ref, gm_ref, sf_ref, scf_ref = (next(it) for _ in range(6))
    if do_conv:
        wdw_ref, bdw_ref = next(it), next(it)
    lng_ref, lnb_ref, wpw_ref, wo_ref, wout_ref, g2_ref = (next(it) for _ in range(6))
    if do_route:
        rwh_ref, rwl_ref, rb_ref = next(it), next(it), next(it)
    xo_ref, h2_ref = next(it), next(it)
    gate_ref = next(it) if do_route else None
    if do_conv:
        xbuf, ybuf, xsh = next(it), next(it), next(it)

    if do_conv:
        i = pl.program_id(1)
        xbuf[0:CONV_HALO, :] = jnp.where(i > 0, halo_ref[0], 0.0)
        xbuf[CONV_HALO:CONV_HALO + tm, :] = u_ref[0]
        off = CONV_HALO - (CONV_W - 1)
        span = tm + CONV_HALO - SUBLANES
        for ph in range(1, SUBLANES):
            xsh[ph - 1] = xbuf[ph:ph + span, :]
        for c0 in range(0, D_CONV, LANES):
            acc = jnp.broadcast_to(bdw_ref[:, c0:c0 + LANES], (tm, LANES))
            for w in range(CONV_W):
                ph, base = (off + w) % SUBLANES, (off + w) // SUBLANES * SUBLANES
                if ph == 0:
                    win = xbuf[base:base + tm, c0:c0 + LANES]
                else:
                    win = xsh[ph - 1, base:base + tm, c0:c0 + LANES]
                acc = acc + win * wdw_ref[w:w + 1, c0:c0 + LANES]
            ybuf[:, c0:c0 + LANES] = acc
        yc = ybuf[...]
    else:
        yc = u_ref[0]

    mu = jnp.mean(yc, axis=-1, keepdims=True)
    var = jnp.mean(jnp.square(yc - mu), axis=-1, keepdims=True)
    yn = (yc - mu) * lax.rsqrt(var + LN_EPS) * lng_ref[...] + lnb_ref[...]
    y_conv = _dot((yn * _sigmoid(yn)).astype(BF16), wpw_ref[...])
    y_attn = _dot(attn_ref[0], wo_ref[...])
    merged = ga_ref[0].astype(F32) * y_attn + gc_ref[0].astype(F32) * y_conv
    xn = x_ref[0] + gm_ref[0] * _dot(merged.astype(BF16), wout_ref[...])
    xo_ref[0] = xn
    ms = jnp.mean(xn * xn, axis=-1, keepdims=True)
    h2 = (xn * lax.rsqrt(ms + NORM_EPS)) * g2_ref[...]
    h2 = h2 * (1.0 + scf_ref[0]) + sf_ref[0]
    h2_ref[0] = h2.astype(BF16)
    if do_route:
        h_hi, h_lo = _split_bf16(h2)
        logits = (_dot(h_hi, rwh_ref[...]) + (_dot(h_lo, rwh_ref[...]) + _dot(h_hi, rwl_ref[...]))) + rb_ref[...]
        eidx = lax.broadcasted_iota(jnp.int32, logits.shape, 1)
        top1 = jnp.max(logits, axis=-1, keepdims=True)
        i1 = jnp.min(jnp.where(logits == top1, eidx, N_EXPERTS), axis=-1, keepdims=True)
        rest = jnp.where(eidx == i1, -jnp.inf, logits)
        top2 = jnp.max(rest, axis=-1, keepdims=True)
        i2 = jnp.min(jnp.where(rest == top2, eidx, N_EXPERTS), axis=-1, keepdims=True)
        e2 = jnp.exp(top2 - top1)
        w1 = 1.0 / (1.0 + e2)
        w2 = e2 / (1.0 + e2)
        gate_ref[0] = jnp.where(eidx == i1, w1, 0.0) + jnp.where(eidx == i2, w2, 0.0)


def _post_mixer(x, u, attn, ga, gc, gate_m, shift_f, scale_f, conv_w, ln_g, ln_b, w_pw_b, w_o_b, w_out_b, g2,
                router, *, tm):
    g, tg, d = x.shape
    r = gate_m.shape[1]
    do_conv = conv_w is not None
    do_route = router is not None
    if r == 1:
        mod_spec = pl.BlockSpec((1, 1, d), lambda b, i: (b, 0, 0))
    else:
        mod_spec = pl.BlockSpec((1, tm, d), lambda b, i: (b, i, 0))
    tok = lambda w: pl.BlockSpec((1, tm, w), lambda b, i: (b, i, 0))
    const2 = lambda b, i: (0, 0)
    args = [x, u]
    specs = [tok(d), tok(D_CONV)]
    if do_conv:
        per = tm // CONV_HALO
        args.append(u)
        specs.append(pl.BlockSpec((1, CONV_HALO, D_CONV), lambda b, i: (b, jnp.maximum(i * per - 1, 0), 0)))
    args += [attn, ga, gc, gate_m, shift_f, scale_f]
    specs += [tok(D_ATTN), tok(d), tok(d), mod_spec, mod_spec, mod_spec]
    if do_conv:
        args += [conv_w[0], conv_w[1]]
        specs += [pl.BlockSpec((CONV_W, D_CONV), const2), pl.BlockSpec((1, D_CONV), const2)]
    args += [ln_g, ln_b, w_pw_b, w_o_b, w_out_b, g2]
    specs += [pl.BlockSpec((1, D_CONV), const2), pl.BlockSpec((1, D_CONV), const2),
              pl.BlockSpec((D_CONV, d), const2), pl.BlockSpec((D_ATTN, d), const2),
              pl.BlockSpec((d, d), const2), pl.BlockSpec((1, d), const2)]
    if do_route:
        args += list(router)
        specs += [pl.BlockSpec((d, N_EXPERTS), const2), pl.BlockSpec((d, N_EXPERTS), const2),
                  pl.BlockSpec((1, N_EXPERTS), const2)]
    out_shape = [jax.ShapeDtypeStruct((g, tg, d), F32), jax.ShapeDtypeStruct((g, tg, d), BF16)]
    out_specs = [tok(d), tok(d)]
    if do_route:
        out_shape.append(jax.ShapeDtypeStruct((g, tg, N_EXPERTS), F32))
        out_specs.append(tok(N_EXPERTS))
    scratch = []
    if do_conv:
        scratch = [pltpu.VMEM((CONV_HALO + tm, D_CONV), F32), pltpu.VMEM((tm, D_CONV), F32),
                   pltpu.VMEM((SUBLANES - 1, tm + CONV_HALO - SUBLANES, D_CONV), F32)]
    outs = pl.pallas_call(
        functools.partial(_post_kernel, tm=tm, do_conv=do_conv, do_route=do_route),
        grid=(g, tg // tm),
        in_specs=specs,
        out_specs=tuple(out_specs),
        out_shape=tuple(out_shape),
        scratch_shapes=scratch,
        compiler_params=_cparams(("parallel", "arbitrary")),
        name="post_mixer",
    )(*args)
    return outs


def _ffn_kernel(h_ref, x_ref, gf_ref, w1_ref, w3_ref, w2_ref, o_ref, acc_ref):
    f = pl.program_id(2)

    @pl.when(f == 0)
    def _():
        acc_ref[...] = jnp.zeros_like(acc_ref)

    hb = h_ref[0]
    a1 = _dot(hb, w1_ref[...])
    a3 = _dot(hb, w3_ref[...])
    act = (a1 * _sigmoid(a1) * a3).astype(BF16)
    acc_ref[...] += _dot(act, w2_ref[...])

    @pl.when(f == pl.num_programs(2) - 1)
    def _():
        o_ref[0] = x_ref[0] + gf_ref[0] * acc_ref[...]


def _ffn_dense(h2, x, gate_f, w1b, w3b, w2b, *, tm, tf):
    g, tg, d = x.shape
    dff = w1b.shape[1]
    tm = min(tm, tg)
    r = gate_f.shape[1]
    if r == 1:
        mod_spec = pl.BlockSpec((1, 1, d), lambda b, i, f: (b, 0, 0))
    else:
        mod_spec = pl.BlockSpec((1, tm, d), lambda b, i, f: (b, i, 0))
    tok = pl.BlockSpec((1, tm, d), lambda b, i, f: (b, i, 0))
    return pl.pallas_call(
        _ffn_kernel,
        grid=(g, tg // tm, dff // tf),
        in_specs=[tok, tok, mod_spec,
                  pl.BlockSpec((d, tf), lambda b, i, f: (0, f)),
                  pl.BlockSpec((d, tf), lambda b, i, f: (0, f)),
                  pl.BlockSpec((tf, d), lambda b, i, f: (f, 0))],
        out_specs=tok,
        out_shape=jax.ShapeDtypeStruct((g, tg, d), F32),
        scratch_shapes=[pltpu.VMEM((tm, d), F32)],
        compiler_params=_cparams(("parallel", "parallel", "arbitrary")),
        name="ffn_dense",
    )(h2, x, gate_f, w1b, w3b, w2b)


MOE_TILE = 256
MOE_ALIGN = 16


def _moe_dispatch_kernel(a_ref, h_ref, rlt_ref, xs_in_ref, xs_ref, cbuf, sem, *, tm, n_e, tile0):
    del xs_in_ref
    i = pl.program_id(0)
    n = pl.num_programs(0)
    slot = i % 2
    hb = h_ref[...]
    kio = lax.broadcasted_iota(jnp.int32, (tm, tm), 0)
    for e in range(n_e):
        pt = jnp.where(kio == rlt_ref[e:e + 1, :], 1.0, 0.0).astype(BF16)
        cbuf[slot, e] = _dot(pt, hb).astype(BF16)

    def copies(step, sl):
        return [pltpu.make_async_copy(
            cbuf.at[sl, e], xs_ref.at[pl.ds(pl.multiple_of(a_ref[(tile0 + step) * n_e + e], MOE_ALIGN), tm)],
            sem.at[sl]) for e in range(n_e)]

    @pl.when(i > 0)
    def _():
        for cp in copies(i - 1, 1 - slot):
            cp.wait()

    for cp in copies(i, slot):
        cp.start()

    @pl.when(i == n - 1)
    def _():
        for cp in copies(i, slot):
            cp.wait()


def _moe_expert_kernel(exp_ref, act_ref, x_ref, w1_ref, w3_ref, w2_ref, yh_ref, yl_ref, acc_ref, *, tf):
    del exp_ref
    i = pl.program_id(0)

    @pl.when(act_ref[i] > 0)
    def _():
        xb = x_ref[...]
        dff = w1_ref.shape[1]
        for f0 in range(0, dff, tf):
            a1 = _dot(xb, w1_ref[:, f0:f0 + tf])
            a3 = _dot(xb, w3_ref[:, f0:f0 + tf])
            part = _dot((a1 * _sigmoid(a1) * a3).astype(BF16), w2_ref[f0:f0 + tf, :])
            if f0 == 0:
                acc_ref[...] = part
            else:
                acc_ref[...] += part
        y_hi, y_lo = _split_bf16(acc_ref[...])
        yh_ref[...] = y_hi
        yl_ref[...] = y_lo

    @pl.when(act_ref[i] == 0)
    def _():
        yh_ref[...] = jnp.zeros(yh_ref.shape, BF16)
        yl_ref[...] = jnp.zeros(yl_ref.shape, BF16)


def _moe_combine_kernel(a_ref, yh_hbm, yl_hbm, x_ref, gf_ref, g_ref, rl_ref, o_ref, ybuf, sem, *, tm, n_e, tile0):
    i = pl.program_id(0)
    n = pl.num_programs(0)
    slot = i % 2

    def copies(step, sl):
        out = []
        for e in range(n_e):
            rows = pl.ds(pl.multiple_of(a_ref[(tile0 + step) * n_e + e], MOE_ALIGN), tm)
            out.append(pltpu.make_async_copy(yh_hbm.at[rows], ybuf.at[sl, 0, e], sem.at[sl]))
            out.append(pltpu.make_async_copy(yl_hbm.at[rows], ybuf.at[sl, 1, e], sem.at[sl]))
        return out

    @pl.when(i == 0)
    def _():
        for cp in copies(0, 0):
            cp.start()

    @pl.when(i + 1 < n)
    def _():
        for cp in copies(i + 1, 1 - slot):
            cp.start()

    for cp in copies(i, slot):
        cp.wait()

    lio = lax.broadcasted_iota(jnp.int32, (tm, tm), 1)
    acc = jnp.zeros(o_ref.shape, F32)
    for e in range(n_e):
        p = jnp.where(lio == rl_ref[:, e:e + 1], 1.0, 0.0).astype(BF16)
        z = _dot(p, ybuf[slot, 0, e]) + _dot(p, ybuf[slot, 1, e])
        acc = acc + g_ref[:, e:e + 1] * z
    o_ref[...] = x_ref[...] + gf_ref[...] * acc


def _ffn_moe_routed(h2s, xs, gfs, gates, w1b, w3b, w2b, *, tf):
    sizes = [x.shape[0] for x in xs]
    t = sum(sizes)
    d = xs[0].shape[1]
    n_e, _, dff = w1b.shape
    tile = MOE_TILE
    tm = next(c for c in (256, 128, 64, 32, 16) if all(sz % c == 0 for sz in sizes))
    nt = t // tm
    ceil_to = lambda v, m: ((v + m - 1) // m) * m
    n_tiles = (2 * t + nt * n_e * (MOE_ALIGN - 1)) // tile + 2 * n_e + 1
    s_pad = n_tiles * tile

    mask = gates > 0.0
    m3 = mask.reshape(nt, tm, n_e).astype(jnp.int32)
    rl = jnp.where(mask, (jnp.cumsum(m3, axis=1) - 1).reshape(t, n_e), -1)
    p_te = ceil_to(jnp.sum(m3, axis=1), MOE_ALIGN)
    act_rows = ceil_to(jnp.sum(p_te, axis=0), tile)
    seg = act_rows + tile
    seg_start = jnp.cumsum(seg) - seg
    a_te = (seg_start[None, :] + jnp.cumsum(p_te, axis=0) - p_te).astype(jnp.int32).reshape(nt * n_e)
    tile_start = jnp.arange(n_tiles, dtype=jnp.int32) * tile
    tile_exp = jnp.minimum(jnp.sum((tile_start[:, None] >= (seg_start + seg)[None, :]).astype(jnp.int32), axis=1),
                           n_e - 1)
    in_seg = tile_start - seg_start[tile_exp]
    tile_act = ((in_seg >= 0) & (in_seg < act_rows[tile_exp])).astype(jnp.int32)
    rl_t = rl.T

    slots = jnp.zeros((s_pad, d), BF16)
    tile0 = 0
    for h2, sz in zip(h2s, sizes):
        slots = pl.pallas_call(
            functools.partial(_moe_dispatch_kernel, tm=tm, n_e=n_e, tile0=tile0),
            grid_spec=pltpu.PrefetchScalarGridSpec(
                num_scalar_prefetch=1, grid=(sz // tm,),
                in_specs=[pl.BlockSpec((tm, d), lambda i, a: (i, 0)),
                          pl.BlockSpec((n_e, tm), lambda i, a, tile0=tile0: (0, tile0 + i)),
                          pl.BlockSpec(memory_space=pl.ANY)],
                out_specs=pl.BlockSpec(memory_space=pl.ANY),
                scratch_shapes=[pltpu.VMEM((2, n_e, tm, d), BF16), pltpu.SemaphoreType.DMA((2,))]),
            out_shape=jax.ShapeDtypeStruct((s_pad, d), BF16),
            input_output_aliases={3: 0},
            compiler_params=_cparams(("arbitrary",)),
            name="moe_dispatch",
        )(a_te, h2, rl_t, slots)
        tile0 += sz // tm

    wspec = lambda shp: pl.BlockSpec((None,) + shp, lambda i, ex, ac: (ex[i], 0, 0))
    slot_spec = pl.BlockSpec((tile, d), lambda i, ex, ac: (i, 0))
    y_hi, y_lo = pl.pallas_call(
        functools.partial(_moe_expert_kernel, tf=tf),
        grid_spec=pltpu.PrefetchScalarGridSpec(
            num_scalar_prefetch=2, grid=(n_tiles,),
            in_specs=[slot_spec, wspec((d, dff)), wspec((d, dff)), wspec((dff, d))],
            out_specs=(slot_spec, slot_spec),
            scratch_shapes=[pltpu.VMEM((tile, d), F32)]),
        out_shape=(jax.ShapeDtypeStruct((s_pad, d), BF16), jax.ShapeDtypeStruct((s_pad, d), BF16)),
        compiler_params=_cparams(("arbitrary",)),
        name="moe_experts",
    )(tile_exp, tile_act, slots, w1b, w3b, w2b)

    outs = []
    tile0 = 0
    for x, gf, sz in zip(xs, gfs, sizes):
        tok = lambda w: pl.BlockSpec((tm, w), lambda i, a: (i, 0))
        tok_g = lambda w, tile0=tile0: pl.BlockSpec((tm, w), lambda i, a: (tile0 + i, 0))
        if gf.ndim == 3:
            per = (sz // gf.shape[0]) // tm
            gf_spec = pl.BlockSpec((None, 1, d), lambda i, a, per=per: (i // per, 0, 0))
        else:
            gf_spec = tok(d)
        outs.append(pl.pallas_call(
            functools.partial(_moe_combine_kernel, tm=tm, n_e=n_e, tile0=tile0),
            grid_spec=pltpu.PrefetchScalarGridSpec(
                num_scalar_prefetch=1, grid=(sz // tm,),
                in_specs=[pl.BlockSpec(memory_space=pl.ANY), pl.BlockSpec(memory_space=pl.ANY),
                          tok(d), gf_spec, tok_g(n_e), tok_g(n_e)],
                out_specs=tok(d),
                scratch_shapes=[pltpu.VMEM((2, 2, n_e, tm, d), BF16), pltpu.SemaphoreType.DMA((2,))]),
            out_shape=jax.ShapeDtypeStruct((sz, d), F32),
            compiler_params=_cparams(("arbitrary",)),
            name="moe_combine",
        )(a_te, y_hi, y_lo, x, gf, gates, rl))
        tile0 += sz // tm
    return outs


def kernel(x_prompt, x_sample, cache_k, cache_v, state_conv, page_table, c_prompt, c_sample, w_mod, b_mod,
           norm1_g, norm2_g, w_in, q_norm_g, k_norm_g, w_o_attn, w_dw, b_dw, conv_ln_g, conv_ln_b, w_pw_conv,
           w_out, ffn_w1, ffn_w3, ffn_w2, router_w, router_b, moe_w1, moe_w3, moe_w2):
    depth = w_in.shape[0]
    n_p, t_p, d = x_prompt.shape
    n_s, t_s, _ = x_sample.shape
    n_tok_s = n_s * t_s
    n_phys = cache_k.shape[1]
    n_blk_p = t_p // MOBA_BLOCK

    c_all = jnp.concatenate([c_prompt, c_sample], axis=0)
    c_all = jnp.pad(c_all, ((0, (-c_all.shape[0]) % 8), (0, 0)))
    mod = _modulation(c_all, w_mod, b_mod)

    cache_kt = cache_k.transpose(0, 1, 3, 4, 2).reshape(depth, n_phys, D_ATTN, PAGE_SIZE)
    cache_vt = cache_v.transpose(0, 1, 3, 4, 2).reshape(depth, n_phys, D_ATTN, PAGE_SIZE)

    head_id = jnp.arange(D_ATTN) // HEAD_DIM
    head_mean = jnp.where(head_id[:, None] == head_id[None, :], 1.0 / HEAD_DIM, 0.0).astype(BF16)

    xp = x_prompt
    xs = x_sample.reshape(1, n_tok_s, d)
    k_p, v_p, conv_p, k_s, v_s, conv_s = [], [], [], [], [], []
    for l in range(depth):
        mod_p = mod[l, :n_p].reshape(n_p, N_MOD, 1, d)
        mod_s = jnp.repeat(mod[l, n_p:n_p + n_s].reshape(n_s, N_MOD, d), t_s, axis=0)
        mp = [mod_p[:, i] for i in range(N_MOD)]
        msm = [mod_s[None, :, i] for i in range(N_MOD)]

        w_in_b = w_in[l].astype(BF16)
        g1 = norm1_g[l].reshape(1, d)
        g2 = norm2_g[l].reshape(1, d)
        gq = jnp.tile(q_norm_g[l], N_HEADS).reshape(1, D_ATTN)
        gk = jnp.tile(k_norm_g[l], N_HEADS).reshape(1, D_ATTN)
        w_o_b = w_o_attn[l].astype(BF16)
        w_pw_b = w_pw_conv[l].astype(BF16)
        w_out_b = w_out[l].astype(BF16)
        ln_g = conv_ln_g[l].reshape(1, D_CONV)
        ln_b = conv_ln_b[l].reshape(1, D_CONV)
        is_moe = l % 2 == 1
        router = None
        if is_moe:
            rw = router_w[l // 2]
            rw_hi = rw.astype(BF16)
            rw_lo = (rw - rw_hi.astype(F32)).astype(BF16)
            router = (rw_hi, rw_lo, router_b[l // 2].reshape(1, N_EXPERTS))

        q_pad, k_aug, v_aug, kf, vf, u, ga, gc, km = _in_projection(
            xp, mp[0], mp[1], g1, w_in_b, head_mean, gq, gk, tm=MOBA_BLOCK, head_major=True)
        kmean = km.reshape(n_p, n_blk_p, N_HEADS, HEAD_DIM).transpose(0, 2, 1, 3)
        kmean = jnp.pad(kmean, ((0, 0), (0, 0), (0, 0), (0, LANES - HEAD_DIM)))
        attn = _moba_prompt(q_pad, k_aug, v_aug, kmean)
        outs = _post_mixer(xp, u, attn, ga, gc, mp[2], mp[3], mp[4], (w_dw[l], b_dw[l].reshape(1, D_CONV)),
                           ln_g, ln_b, w_pw_b, w_o_b, w_out_b, g2, router, tm=256)
        k_p.append(kf.reshape(n_p, t_p, N_HEADS, HEAD_DIM))
        v_p.append(vf.reshape(n_p, t_p, N_HEADS, HEAD_DIM))
        conv_p.append(u[:, t_p - (CONV_W - 1):])
        xp_mid, h2_p = outs[0], outs[1]
        gates_p = outs[2] if is_moe else None

        qs, kfs, vfs, us, gas, gcs = _in_projection(
            xs, msm[0], msm[1], g1, w_in_b, head_mean, gq, gk, tm=min(256, n_tok_s), head_major=False)
        seq = lambda a: a.reshape(n_s, t_s, a.shape[-1])
        attn_s = _moba_sample(page_table, seq(qs), seq(kfs), seq(vfs), cache_kt, cache_vt, l)
        attn_s = attn_s.reshape(1, n_tok_s, D_ATTN).astype(BF16)
        u_s = seq(us)
        y_dw = _conv_sample(state_conv[l], u_s, w_dw[l], b_dw[l]).reshape(1, n_tok_s, D_CONV)
        outs = _post_mixer(xs, y_dw, attn_s, gas, gcs, msm[2], msm[3], msm[4], None,
                           ln_g, ln_b, w_pw_b, w_o_b, w_out_b, g2, router, tm=min(256, n_tok_s))
        k_s.append(kfs.reshape(n_s, t_s, N_HEADS, HEAD_DIM))
        v_s.append(vfs.reshape(n_s, t_s, N_HEADS, HEAD_DIM))
        conv_s.append(jnp.concatenate([state_conv[l], u_s], axis=1)[:, -(CONV_W - 1):])
        xs_mid, h2_s = outs[0], outs[1]
        gates_s = outs[2] if is_moe else None

        if is_moe:
            w1b = moe_w1[l // 2].astype(BF16)
            w3b = moe_w3[l // 2].astype(BF16)
            w2b = moe_w2[l // 2].astype(BF16)
            n_tok_p = n_p * t_p
            gates_all = jnp.concatenate([gates_p.reshape(n_tok_p, N_EXPERTS), gates_s.reshape(n_tok_s, N_EXPERTS)])
            xp, xs = _ffn_moe_routed(
                [h2_p.reshape(n_tok_p, d), h2_s.reshape(n_tok_s, d)],
                [xp_mid.reshape(n_tok_p, d), xs_mid.reshape(n_tok_s, d)],
                [mp[5], msm[5].reshape(n_tok_s, d)], gates_all, w1b, w3b, w2b, tf=1408)
            xp = xp.reshape(n_p, t_p, d)
            xs = xs.reshape(1, n_tok_s, d)
        else:
            w1b = ffn_w1[l // 2].astype(BF16)
            w3b = ffn_w3[l // 2].astype(BF16)
            w2b = ffn_w2[l // 2].astype(BF16)
            xp = _ffn_dense(h2_p, xp_mid, mp[5], w1b, w3b, w2b, tm=512, tf=1408)
            xs = _ffn_dense(h2_s, xs_mid, msm[5], w1b, w3b, w2b, tm=512, tf=1408)

    return (xp, xs.reshape(n_s, t_s, d), jnp.stack(k_p), jnp.stack(v_p), jnp.stack(conv_p),
            jnp.stack(k_s), jnp.stack(v_s), jnp.stack(conv_s))
```

```python
import functools

import jax
import jax.numpy as jnp
from jax import lax
from jax.experimental import pallas as pl
from jax.experimental.pallas import tpu as pltpu

F32 = jnp.float32
BF16 = jnp.bfloat16
HIGHEST = lax.Precision.HIGHEST

D_MODEL = 1024
N_HEADS = 8
HEAD_DIM = 64
D_ATTN = N_HEADS * HEAD_DIM
MOBA_BLOCK = 256
MOBA_TOPK = 3
D_CONV = 512
CONV_W = 31
N_EXPERTS = 8
N_MOD = 6
PAGE_SIZE = 128
NORM_EPS = 1e-6
LN_EPS = 1e-5
NEG = -1e30
LANES = 128
SUBLANES = 8
CONV_HALO = 32
VMEM_LIMIT = 56 * 1024 * 1024


def _cparams(sem):
    return pltpu.CompilerParams(dimension_semantics=sem, vmem_limit_bytes=VMEM_LIMIT)


def _dot(a, b):
    return jnp.dot(a, b, preferred_element_type=F32)


def _dot_nt(a, b, precision=None):
    return lax.dot_general(a, b, (((1,), (1,)), ((), ())), precision=precision, preferred_element_type=F32)


def _sigmoid(x):
    return 1.0 / (1.0 + jnp.exp(-x))


def _split_bf16(x):
    hi = x.astype(BF16)
    lo = (x - hi.astype(F32)).astype(BF16)
    return hi, lo


def _top_select(score, valid, idx, n_sel, axis, sentinel):
    remaining = valid
    sel = jnp.zeros(score.shape, dtype=jnp.bool_)
    for _ in range(n_sel):
        cur = jnp.where(remaining, score, -jnp.inf)
        best = jnp.max(cur, axis=axis, keepdims=True)
        cand = remaining & (cur == best)
        first = jnp.min(jnp.where(cand, idx, sentinel), axis=axis, keepdims=True)
        pick = idx == first
        sel = sel | pick
        remaining = remaining & jnp.logical_not(pick)
    return sel


def _mod_kernel(c_ref, w_ref, b_ref, o_ref):
    c = c_ref[...]
    cond = c * _sigmoid(c)
    o_ref[0] = _dot(cond.astype(BF16), w_ref[0].astype(BF16)) + b_ref[0]


def _modulation(c_all, w_mod, b_mod):
    depth, d, n = w_mod.shape
    rows = c_all.shape[0]
    tn = 1536
    return pl.pallas_call(
        _mod_kernel,
        grid=(depth, n // tn),
        in_specs=[
            pl.BlockSpec((rows, d), lambda l, j: (0, 0)),
            pl.BlockSpec((1, d, tn), lambda l, j: (l, 0, j)),
            pl.BlockSpec((1, 1, tn), lambda l, j: (l, 0, j)),
        ],
        out_specs=pl.BlockSpec((1, rows, tn), lambda l, j: (l, 0, j)),
        out_shape=jax.ShapeDtypeStruct((depth, rows, n), F32),
        compiler_params=_cparams(("parallel", "parallel")),
        name="adaln_modulation",
    )(c_all, w_mod, b_mod.reshape(depth, 1, n))


def _inproj_kernel(x_ref, shift_ref, scale_ref, g1_ref, w_ref, hm_ref, gq_ref, gk_ref, *out_refs, tm, head_major):
    if head_major:
        qp_ref, ka_ref, va_ref, kf_ref, vf_ref, u_ref, ga_ref, gc_ref, km_ref = out_refs
    else:
        qf_ref, kf_ref, vf_ref, u_ref, ga_ref, gc_ref = out_refs
    x = x_ref[0]
    ms = jnp.mean(x * x, axis=-1, keepdims=True)
    h = (x * lax.rsqrt(ms + NORM_EPS)) * g1_ref[...]
    h = h * (1.0 + scale_ref[0]) + shift_ref[0]
    hb = h.astype(BF16)

    def proj(lo, hi):
        return _dot(hb, w_ref[:, lo:hi])

    def head_rms(z, g_ref):
        z2_hi, z2_lo = _split_bf16(z * z)
        msh = _dot(z2_hi, hm_ref[...]) + _dot(z2_lo, hm_ref[...])
        return z * lax.rsqrt(msh + NORM_EPS) * g_ref[...]

    b0, b1, b2, b3, b4, b5, b6 = (0, D_ATTN, 2 * D_ATTN, 3 * D_ATTN, 3 * D_ATTN + D_CONV,
                                  3 * D_ATTN + 2 * D_CONV, 3 * D_ATTN + 2 * D_CONV + D_MODEL)
    q = head_rms(proj(b0, b1), gq_ref)
    k = head_rms(proj(b1, b2), gk_ref)
    v = proj(b2, b3)
    kf_ref[0] = k
    vf_ref[0] = v
    if head_major:
        blk = pl.program_id(1) * (tm // MOBA_BLOCK)
        lane = lax.broadcasted_iota(jnp.int32, (tm, LANES), 1)
        low = lane < HEAD_DIM
        k_tail = jnp.where(lane == HEAD_DIM + blk, 1.0, 0.0)
        v_tail = jnp.where(lane == HEAD_DIM, 1.0, 0.0)
        for hp in range(N_HEADS // 2):
            sl = slice(hp * LANES, (hp + 1) * LANES)
            for z, dst, tail in ((q, qp_ref, 0.0), (k, ka_ref, k_tail), (v, va_ref, v_tail)):
                pair = z[:, sl]
                for hh, head in ((2 * hp, pair), (2 * hp + 1, pltpu.roll(pair, HEAD_DIM, 1))):
                    padded = jnp.where(low, head, tail)
                    if dst is va_ref:
                        dst[0, hh, 0] = jnp.transpose(padded).astype(dst.dtype)
                    else:
                        dst[0, hh] = padded.astype(dst.dtype)
        km_ref[0, 0] = jnp.mean(k, axis=0, keepdims=True)
    else:
        qf_ref[0] = q
    u_ref[0] = proj(b3, b4) * _sigmoid(proj(b4, b5))
    ga_ref[0] = _sigmoid(proj(b5, b6)).astype(BF16)
    gc_ref[0] = _sigmoid(proj(b6, b6 + D_MODEL)).astype(BF16)


def _in_projection(x, shift, scale, g1, w_in_b, head_mean, gq, gk, *, tm, head_major):
    g, tg, d = x.shape
    r = shift.shape[1]
    n_in = w_in_b.shape[1]
    if r == 1:
        mod_spec = pl.BlockSpec((1, 1, d), lambda b, i: (b, 0, 0))
    else:
        mod_spec = pl.BlockSpec((1, tm, d), lambda b, i: (b, i, 0))
    const2 = lambda b, i: (0, 0)
    tok = lambda w: pl.BlockSpec((1, tm, w), lambda b, i: (b, i, 0))
    tok_shape = lambda w, dt: jax.ShapeDtypeStruct((g, tg, w), dt)
    common_shapes = (tok_shape(D_ATTN, F32), tok_shape(D_ATTN, F32), tok_shape(D_CONV, F32),
                     tok_shape(d, BF16), tok_shape(d, BF16))
    common_specs = (tok(D_ATTN), tok(D_ATTN), tok(D_CONV), tok(d), tok(d))
    if head_major:
        assert tm == MOBA_BLOCK
        heads = pl.BlockSpec((1, N_HEADS, tm, LANES), lambda b, i: (b, 0, i, 0))
        hshape = lambda dt: jax.ShapeDtypeStruct((g, N_HEADS, tg, LANES), dt)
        vt_shape = jax.ShapeDtypeStruct((g, N_HEADS, tg // (2 * tm), LANES, 2 * tm), BF16)
        vt_spec = pl.BlockSpec((1, N_HEADS, 1, LANES, tm), lambda b, i: (b, 0, i // 2, 0, i % 2))
        out_shapes = (hshape(F32), hshape(BF16), vt_shape) + common_shapes + (
            jax.ShapeDtypeStruct((g, tg // MOBA_BLOCK, 1, D_ATTN), F32),)
        out_specs = (heads, heads, vt_spec) + common_specs + (
            pl.BlockSpec((1, 1, 1, D_ATTN), lambda b, i: (b, i, 0, 0)),)
    else:
        out_shapes = (tok_shape(D_ATTN, F32),) + common_shapes
        out_specs = (tok(D_ATTN),) + common_specs
    return pl.pallas_call(
        functools.partial(_inproj_kernel, tm=tm, head_major=head_major),
        grid=(g, tg // tm),
        in_specs=[tok(d), mod_spec, mod_spec,
                  pl.BlockSpec((1, d), const2),
                  pl.BlockSpec((d, n_in), const2),
                  pl.BlockSpec((D_ATTN, D_ATTN), const2),
                  pl.BlockSpec((1, D_ATTN), const2),
                  pl.BlockSpec((1, D_ATTN), const2)],
        out_specs=out_specs,
        out_shape=out_shapes,
        compiler_params=_cparams(("parallel", "parallel")),
        name="in_projection",
    )(x, shift, scale, g1, w_in_b, head_mean, gq, gk)


def _moba_prompt_kernel(q_ref, k_ref, vt_ref, km_ref, o_ref, qa_scr, s_scr, m_scr, acc_scr, *, n_blk, hps):
    j = pl.program_id(2)
    tq = MOBA_BLOCK
    tk = 2 * MOBA_BLOCK
    own_pair = j // 2
    lane = lax.broadcasted_iota(jnp.int32, (tq, LANES), 1)
    blk_t = lax.broadcasted_iota(jnp.int32, (n_blk, tq), 0)

    def group_max(s):
        return jnp.max(s.reshape(tk // SUBLANES, SUBLANES, tq), axis=0)

    for hh in range(hps):
        qp = q_ref[0, hh]
        gate_t = _dot_nt(km_ref[0, hh], qp, precision=HIGHEST)
        sel_t = _top_select(gate_t, blk_t < j, blk_t, min(MOBA_TOPK, n_blk), 0, n_blk)
        bias_t = jnp.where(sel_t | (blk_t == j), 0.0, NEG)
        bias_rows = jnp.concatenate([jnp.zeros((HEAD_DIM, tq), F32), bias_t,
                                     jnp.zeros((LANES - HEAD_DIM - n_blk, tq), F32)], axis=0)
        qa_scr[hh] = (jnp.transpose(qp) * (HEAD_DIM ** -0.5) + bias_rows).astype(BF16)

    own_off = (j & 1) * tq
    key = lax.broadcasted_iota(jnp.int32, (tk, tq), 0)
    qry = lax.broadcasted_iota(jnp.int32, (tk, tq), 1)
    hidden = key - own_off > qry
    st_own = pl.multiple_of(own_pair * tk, tk)
    for hh in range(hps):
        s = _dot(k_ref[0, hh, pl.ds(st_own, tk), :], qa_scr[hh])
        s = jnp.where(hidden, NEG, s)
        s_scr[hh, own_pair] = s
        m_scr[hh] = group_max(s)

    def pass1(ip, carry):
        st = pl.multiple_of(ip * tk, tk)
        for hh in range(hps):
            s = _dot(k_ref[0, hh, pl.ds(st, tk), :], qa_scr[hh])
            s_scr[hh, ip] = s
            m_scr[hh] = jnp.maximum(m_scr[hh], group_max(s))
        return carry

    lax.fori_loop(0, own_pair, pass1, 0)
    for hh in range(hps):
        m = jnp.max(m_scr[hh], axis=0, keepdims=True)
        m_scr[hh] = jnp.broadcast_to(m, (SUBLANES, tq))
        acc_scr[hh] = jnp.zeros((LANES, tq), F32)

    def pass2(ip, carry):
        for hh in range(hps):
            p = jnp.exp(s_scr[hh, ip] - m_scr[hh][0:1, :]).astype(BF16)
            acc_scr[hh] += _dot(vt_ref[0, hh, ip], p)
        return carry

    lax.fori_loop(0, own_pair + 1, pass2, 0)
    outs = []
    for hh in range(hps):
        acc = acc_scr[hh]
        outs.append(jnp.transpose(acc / acc[HEAD_DIM:HEAD_DIM + 1, :]))
    slabs = [jnp.where(lane < HEAD_DIM, outs[2 * i], pltpu.roll(outs[2 * i + 1], HEAD_DIM, 1))
             for i in range(hps // 2)]
    o_ref[0] = jnp.concatenate(slabs, axis=-1).astype(o_ref.dtype)


def _moba_prompt(q_pad, k_aug, v_aug_t, kmean_pad, *, hps=4):
    b, h, t, _ = q_pad.shape
    n_blk = t // MOBA_BLOCK
    assert HEAD_DIM + n_blk <= LANES and n_blk % 2 == 0 and h % hps == 0 and hps % 2 == 0
    tq = MOBA_BLOCK
    return pl.pallas_call(
        functools.partial(_moba_prompt_kernel, n_blk=n_blk, hps=hps),
        grid=(b, h // hps, n_blk),
        in_specs=[pl.BlockSpec((1, hps, tq, LANES), lambda bi, hg, j: (bi, hg, j, 0)),
                  pl.BlockSpec((1, hps, t, LANES), lambda bi, hg, j: (bi, hg, 0, 0)),
                  pl.BlockSpec((1, hps, n_blk // 2, LANES, 2 * tq), lambda bi, hg, j: (bi, hg, 0, 0, 0)),
                  pl.BlockSpec((1, hps, n_blk, LANES), lambda bi, hg, j: (bi, hg, 0, 0))],
        out_specs=pl.BlockSpec((1, tq, hps * HEAD_DIM), lambda bi, hg, j: (bi, j, hg)),
        out_shape=jax.ShapeDtypeStruct((b, t, h * HEAD_DIM), BF16),
        scratch_shapes=[pltpu.VMEM((hps, LANES, tq), BF16),
                        pltpu.VMEM((hps, n_blk // 2, 2 * tq, tq), F32),
                        pltpu.VMEM((hps, SUBLANES, tq), F32),
                        pltpu.VMEM((hps, LANES, tq), F32)],
        compiler_params=_cparams(("parallel", "parallel", "arbitrary")),
        name="moba_prompt_attention",
    )(q_pad, k_aug, v_aug_t, kmean_pad)


def _moba_sample_kernel(pt_ref, q_ref, knew_ref, vnew_ref, *refs, n_pages, n_q):
    del pt_ref
    k_refs = refs[:n_pages]
    v_refs = refs[n_pages:2 * n_pages]
    o_ref = refs[2 * n_pages]
    s_scr = refs[2 * n_pages + 1]
    n_row = n_q * N_HEADS
    pages_per_blk = MOBA_BLOCK // PAGE_SIZE
    n_blk = n_pages // pages_per_blk
    h_shift = N_HEADS.bit_length() - 1
    d_shift = HEAD_DIM.bit_length() - 1
    scale = HEAD_DIM ** -0.5

    q = q_ref[0]
    q_rep = jnp.concatenate([jnp.broadcast_to(q[i:i + 1, :], (N_HEADS, D_ATTN)) for i in range(n_q)], axis=0)
    r_i = lax.broadcasted_iota(jnp.int32, (n_row, D_ATTN), 0)
    c_i = lax.broadcasted_iota(jnp.int32, (n_row, D_ATTN), 1)
    own_head = (r_i & (N_HEADS - 1)) == (c_i >> d_shift)
    q_bd = jnp.where(own_head, q_rep, 0.0)
    q_hi, q_lo = _split_bf16(q_bd)

    lane = lax.broadcasted_iota(jnp.int32, (n_row, PAGE_SIZE), 1)
    q_both = jnp.concatenate([q_hi, q_lo], axis=0)
    block_of = lambda refs, bi: jnp.concatenate(
        [refs[bi * pages_per_blk + pp][...] for pp in range(pages_per_blk)], axis=1)
    gate = jnp.zeros((n_row, PAGE_SIZE), F32)
    for bi in range(n_blk):
        k_hi, k_lo = _split_bf16(block_of(k_refs, bi))
        both = _dot(q_both, k_hi)
        s = both[:n_row] + (both[n_row:] + _dot(q_hi, k_lo))
        s_scr[bi] = s
        tot = jnp.sum(s, axis=-1, keepdims=True)
        gate = gate + jnp.where(lane == bi, tot, 0.0)
    sel = _top_select(gate * (1.0 / MOBA_BLOCK), lane < n_blk, lane, min(MOBA_TOPK, n_blk + 1), 1, PAGE_SIZE)
    sel_f = jnp.where(sel, 1.0, 0.0)

    knew = knew_ref[0]
    vnew = vnew_ref[0]
    q_idx = lax.broadcasted_iota(jnp.int32, (n_row, 1), 0) >> h_shift
    s_new = [jnp.where(q_idx >= jj, jnp.sum(q_bd * knew[jj:jj + 1, :], axis=-1, keepdims=True) * scale, NEG)
             for jj in range(n_q)]
    m = s_new[0]
    for jj in range(1, n_q):
        m = jnp.maximum(m, s_new[jj])
    chosen = [jnp.max(jnp.where(lane == bi, sel_f, 0.0), axis=-1, keepdims=True) > 0.0 for bi in range(n_blk)]
    mpart = jnp.full((n_row, MOBA_BLOCK), NEG, F32)
    for bi in range(n_blk):
        s = jnp.where(chosen[bi], s_scr[bi] * scale, NEG)
        s_scr[bi] = s
        mpart = jnp.maximum(mpart, s)
    m = jnp.maximum(m, jnp.max(mpart, axis=-1, keepdims=True))

    acc = jnp.zeros((n_row, D_ATTN), F32)
    l = jnp.zeros((n_row, 1), F32)
    for jj in range(n_q):
        p = jnp.exp(s_new[jj] - m)
        l = l + p
        acc = acc + p * vnew[jj:jj + 1, :]
    lpart = jnp.zeros((n_row, MOBA_BLOCK), F32)
    for bi in range(n_blk):
        p = jnp.exp(s_scr[bi] - m)
        lpart = lpart + p
        acc = acc + _dot_nt(p.astype(BF16), block_of(v_refs, bi).astype(BF16))
    l = l + jnp.sum(lpart, axis=-1, keepdims=True)
    out = jnp.where(own_head, acc / l, 0.0).reshape(n_q, N_HEADS, D_ATTN)
    o_ref[0] = jnp.sum(out, axis=1)


def _moba_sample(page_table, q, k_new, v_new, cache_kt, cache_vt, layer):
    n_seq, n_pages = page_table.shape
    n_q = q.shape[1]
    seq3 = pl.BlockSpec((1, n_q, D_ATTN), lambda b, pt: (b, 0, 0))

    def page_spec(pg):
        return pl.BlockSpec((None, None, D_ATTN, PAGE_SIZE), lambda b, pt, pg=pg: (layer, pt[b, pg], 0, 0))

    in_specs = [seq3, seq3, seq3] + [page_spec(pg) for pg in range(n_pages)] * 2
    grid_spec = pltpu.PrefetchScalarGridSpec(
        num_scalar_prefetch=1, grid=(n_seq,), in_specs=in_specs, out_specs=seq3,
        scratch_shapes=[pltpu.VMEM((n_pages * PAGE_SIZE // MOBA_BLOCK, n_q * N_HEADS, MOBA_BLOCK), F32)])
    return pl.pallas_call(
        functools.partial(_moba_sample_kernel, n_pages=n_pages, n_q=n_q),
        grid_spec=grid_spec,
        out_shape=jax.ShapeDtypeStruct((n_seq, n_q, D_ATTN), F32),
        compiler_params=_cparams(("arbitrary",)),
        name="moba_sample_attention",
    )(page_table, q, k_new, v_new, *([cache_kt] * n_pages), *([cache_vt] * n_pages))


def _conv_sample_kernel(st_ref, u_ref, wst_ref, wu_ref, b_ref, o_ref, *, n_q):
    st = st_ref[...]
    u = u_ref[...]
    for t in range(n_q):
        y = jnp.sum(st * wst_ref[t], axis=1) + jnp.sum(u * wu_ref[t], axis=1)
        o_ref[t] = y + b_ref[...]


def _conv_sample(state, u, w_dw, b_dw):
    n_seq, ctx, c = state.shape
    n_q = u.shape[1]
    r = jnp.arange(ctx)[None, :] - jnp.arange(n_q)[:, None]
    w_state = jnp.where((r >= 0)[..., None], w_dw[jnp.clip(r, 0, CONV_W - 1)], 0.0)
    ju = ctx + jnp.arange(n_q)[None, :] - jnp.arange(n_q)[:, None]
    w_new = jnp.where((ju <= ctx)[..., None], w_dw[jnp.clip(ju, 0, CONV_W - 1)], 0.0)
    sb = 16
    out = pl.pallas_call(
        functools.partial(_conv_sample_kernel, n_q=n_q),
        grid=(n_seq // sb,),
        in_specs=[pl.BlockSpec((sb, ctx, c), lambda i: (i, 0, 0)),
                  pl.BlockSpec((sb, n_q, c), lambda i: (i, 0, 0)),
                  pl.BlockSpec((n_q, ctx, c), lambda i: (0, 0, 0)),
                  pl.BlockSpec((n_q, n_q, c), lambda i: (0, 0, 0)),
                  pl.BlockSpec((1, c), lambda i: (0, 0))],
        out_specs=pl.BlockSpec((n_q, sb, c), lambda i: (0, i, 0)),
        out_shape=jax.ShapeDtypeStruct((n_q, n_seq, c), F32),
        compiler_params=_cparams(("parallel",)),
        name="conv_sample",
    )(state, u, w_state, w_new, b_dw.reshape(1, c))
    return out.transpose(1, 0, 2)


def _post_kernel(*refs, tm, do_conv, do_route):
    it = iter(refs)
    x_ref, u_ref = next(it), next(it)
    halo_ref = next(it) if do_conv else None
    attn_ref, ga_ref, gc_ref, gm_ref, sf_ref, scf_ref = (next(it) for _ in range(6))
    if do_conv:
        wdw_ref, bdw_ref = next(it), next(it)
    lng_ref, lnb_ref, wpw_ref, wo_ref, wout_ref, g2_ref = (next(it) for _ in range(6))
    if do_route:
        rwh_ref, rwl_ref, rb_ref = next(it), next(it), next(it)
    xo_ref, h2_ref = next(it), next(it)
    gate_ref = next(it) if do_route else None
    if do_conv:
        xbuf, ybuf, xsh = next(it), next(it), next(it)

    if do_conv:
        i = pl.program_id(1)
        xbuf[0:CONV_HALO, :] = jnp.where(i > 0, halo_ref[0], 0.0)
        xbuf[CONV_HALO:CONV_HALO + tm, :] = u_ref[0]
        off = CONV_HALO - (CONV_W - 1)
        span = tm + CONV_HALO - SUBLANES
        for ph in range(1, SUBLANES):
            xsh[ph - 1] = xbuf[ph:ph + span, :]
        for c0 in range(0, D_CONV, LANES):
            acc = jnp.broadcast_to(bdw_ref[:, c0:c0 + LANES], (tm, LANES))
            for w in range(CONV_W):
                ph, base = (off + w) % SUBLANES, (off + w) // SUBLANES * SUBLANES
                if ph == 0:
                    win = xbuf[base:base + tm, c0:c0 + LANES]
                else:
                    win = xsh[ph - 1, base:base + tm, c0:c0 + LANES]
                acc = acc + win * wdw_ref[w:w + 1, c0:c0 + LANES]
            ybuf[:, c0:c0 + LANES] = acc
        yc = ybuf[...]
    else:
        yc = u_ref[0]

    mu = jnp.mean(yc, axis=-1, keepdims=True)
    var = jnp.mean(jnp.square(yc - mu), axis=-1, keepdims=True)
    yn = (yc - mu) * lax.rsqrt(var + LN_EPS) * lng_ref[...] + lnb_ref[...]
    y_conv = _dot((yn * _sigmoid(yn)).astype(BF16), wpw_ref[...])
    y_attn = _dot(attn_ref[0], wo_ref[...])
    merged = ga_ref[0].astype(F32) * y_attn + gc_ref[0].astype(F32) * y_conv
    xn = x_ref[0] + gm_ref[0] * _dot(merged.astype(BF16), wout_ref[...])
    xo_ref[0] = xn
    ms = jnp.mean(xn * xn, axis=-1, keepdims=True)
    h2 = (xn * lax.rsqrt(ms + NORM_EPS)) * g2_ref[...]
    h2 = h2 * (1.0 + scf_ref[0]) + sf_ref[0]
    h2_ref[0] = h2.astype(BF16)
    if do_route:
        h_hi, h_lo = _split_bf16(h2)
        logits = (_dot(h_hi, rwh_ref[...]) + (_dot(h_lo, rwh_ref[...]) + _dot(h_hi, rwl_ref[...]))) + rb_ref[...]
        eidx = lax.broadcasted_iota(jnp.int32, logits.shape, 1)
        top1 = jnp.max(logits, axis=-1, keepdims=True)
        i1 = jnp.min(jnp.where(logits == top1, eidx, N_EXPERTS), axis=-1, keepdims=True)
        rest = jnp.where(eidx == i1, -jnp.inf, logits)
        top2 = jnp.max(rest, axis=-1, keepdims=True)
        i2 = jnp.min(jnp.where(rest == top2, eidx, N_EXPERTS), axis=-1, keepdims=True)
        e2 = jnp.exp(top2 - top1)
        w1 = 1.0 / (1.0 + e2)
        w2 = e2 / (1.0 + e2)
        gate_ref[0] = jnp.where(eidx == i1, w1, 0.0) + jnp.where(eidx == i2, w2, 0.0)


def _post_mixer(x, u, attn, ga, gc, gate_m, shift_f, scale_f, conv_w, ln_g, ln_b, w_pw_b, w_o_b, w_out_b, g2,
                router, *, tm):
    g, tg, d = x.shape
    r = gate_m.shape[1]
    do_conv = conv_w is not None
    do_route = router is not None
    if r == 1:
        mod_spec = pl.BlockSpec((1, 1, d), lambda b, i: (b, 0, 0))
    else:
        mod_spec = pl.BlockSpec((1, tm, d), lambda b, i: (b, i, 0))
    tok = lambda w: pl.BlockSpec((1, tm, w), lambda b, i: (b, i, 0))
    const2 = lambda b, i: (0, 0)
    args = [x, u]
    specs = [tok(d), tok(D_CONV)]
    if do_conv:
        per = tm // CONV_HALO
        args.append(u)
        specs.append(pl.BlockSpec((1, CONV_HALO, D_CONV), lambda b, i: (b, jnp.maximum(i * per - 1, 0), 0)))
    args += [attn, ga, gc, gate_m, shift_f, scale_f]
    specs += [tok(D_ATTN), tok(d), tok(d), mod_spec, mod_spec, mod_spec]
    if do_conv:
        args += [conv_w[0], conv_w[1]]
        specs += [pl.BlockSpec((CONV_W, D_CONV), const2), pl.BlockSpec((1, D_CONV), const2)]
    args += [ln_g, ln_b, w_pw_b, w_o_b, w_out_b, g2]
    specs += [pl.BlockSpec((1, D_CONV), const2), pl.BlockSpec((1, D_CONV), const2),
              pl.BlockSpec((D_CONV, d), const2), pl.BlockSpec((D_ATTN, d), const2),
              pl.BlockSpec((d, d), const2), pl.BlockSpec((1, d), const2)]
    if do_route:
        args += list(router)
        specs += [pl.BlockSpec((d, N_EXPERTS), const2), pl.BlockSpec((d, N_EXPERTS), const2),
                  pl.BlockSpec((1, N_EXPERTS), const2)]
    out_shape = [jax.ShapeDtypeStruct((g, tg, d), F32), jax.ShapeDtypeStruct((g, tg, d), BF16)]
    out_specs = [tok(d), tok(d)]
    if do_route:
        out_shape.append(jax.ShapeDtypeStruct((g, tg, N_EXPERTS), F32))
        out_specs.append(tok(N_EXPERTS))
    scratch = []
    if do_conv:
        scratch = [pltpu.VMEM((CONV_HALO + tm, D_CONV), F32), pltpu.VMEM((tm, D_CONV), F32),
                   pltpu.VMEM((SUBLANES - 1, tm + CONV_HALO - SUBLANES, D_CONV), F32)]
    outs = pl.pallas_call(
        functools.partial(_post_kernel, tm=tm, do_conv=do_conv, do_route=do_route),
        grid=(g, tg // tm),
        in_specs=specs,
        out_specs=tuple(out_specs),
        out_shape=tuple(out_shape),
        scratch_shapes=scratch,
        compiler_params=_cparams(("parallel", "arbitrary")),
        name="post_mixer",
    )(*args)
    return outs


def _ffn_kernel(h_ref, x_ref, gf_ref, w1_ref, w3_ref, w2_ref, o_ref, acc_ref):
    f = pl.program_id(2)

    @pl.when(f == 0)
    def _():
        acc_ref[...] = jnp.zeros_like(acc_ref)

    hb = h_ref[0]
    a1 = _dot(hb, w1_ref[...])
    a3 = _dot(hb, w3_ref[...])
    act = (a1 * _sigmoid(a1) * a3).astype(BF16)
    acc_ref[...] += _dot(act, w2_ref[...])

    @pl.when(f == pl.num_programs(2) - 1)
    def _():
        o_ref[0] = x_ref[0] + gf_ref[0] * acc_ref[...]


def _ffn_dense(h2, x, gate_f, w1b, w3b, w2b, *, tm, tf):
    g, tg, d = x.shape
    dff = w1b.shape[1]
    tm = min(tm, tg)
    r = gate_f.shape[1]
    if r == 1:
        mod_spec = pl.BlockSpec((1, 1, d), lambda b, i, f: (b, 0, 0))
    else:
        mod_spec = pl.BlockSpec((1, tm, d), lambda b, i, f: (b, i, 0))
    tok = pl.BlockSpec((1, tm, d), lambda b, i, f: (b, i, 0))
    return pl.pallas_call(
        _ffn_kernel,
        grid=(g, tg // tm, dff // tf),
        in_specs=[tok, tok, mod_spec,
                  pl.BlockSpec((d, tf), lambda b, i, f: (0, f)),
                  pl.BlockSpec((d, tf), lambda b, i, f: (0, f)),
                  pl.BlockSpec((tf, d), lambda b, i, f: (f, 0))],
        out_specs=tok,
        out_shape=jax.ShapeDtypeStruct((g, tg, d), F32),
        scratch_shapes=[pltpu.VMEM((tm, d), F32)],
        compiler_params=_cparams(("parallel", "parallel", "arbitrary")),
        name="ffn_dense",
    )(h2, x, gate_f, w1b, w3b, w2b)


MOE_TILE = 256
MOE_ALIGN = 16


def _moe_dispatch_kernel(a_ref, h_ref, rlt_ref, xs_in_ref, xs_ref, cbuf, sem, *, tm, n_e, tile0):
    del xs_in_ref
    i = pl.program_id(0)
    n = pl.num_programs(0)
    slot = i % 2
    hb = h_ref[...]
    kio = lax.broadcasted_iota(jnp.int32, (tm, tm), 0)
    for e in range(n_e):
        pt = jnp.where(kio == rlt_ref[e:e + 1, :], 1.0, 0.0).astype(BF16)
        cbuf[slot, e] = _dot(pt, hb).astype(BF16)

    def copies(step, sl):
        return [pltpu.make_async_copy(
            cbuf.at[sl, e], xs_ref.at[pl.ds(pl.multiple_of(a_ref[(tile0 + step) * n_e + e], MOE_ALIGN), tm)],
            sem.at[sl]) for e in range(n_e)]

    @pl.when(i > 0)
    def _():
        for cp in copies(i - 1, 1 - slot):
            cp.wait()

    for cp in copies(i, slot):
        cp.start()

    @pl.when(i == n - 1)
    def _():
        for cp in copies(i, slot):
            cp.wait()


def _moe_expert_kernel(exp_ref, act_ref, x_ref, w1_ref, w3_ref, w2_ref, yh_ref, yl_ref, acc_ref, *, tf):
    del exp_ref
    i = pl.program_id(0)

    @pl.when(act_ref[i] > 0)
    def _():
        xb = x_ref[...]
        dff = w1_ref.shape[1]
        for f0 in range(0, dff, tf):
            a1 = _dot(xb, w1_ref[:, f0:f0 + tf])
            a3 = _dot(xb, w3_ref[:, f0:f0 + tf])
            part = _dot((a1 * _sigmoid(a1) * a3).astype(BF16), w2_ref[f0:f0 + tf, :])
            if f0 == 0:
                acc_ref[...] = part
            else:
                acc_ref[...] += part
        y_hi, y_lo = _split_bf16(acc_ref[...])
        yh_ref[...] = y_hi
        yl_ref[...] = y_lo

    @pl.when(act_ref[i] == 0)
    def _():
        yh_ref[...] = jnp.zeros(yh_ref.shape, BF16)
        yl_ref[...] = jnp.zeros(yl_ref.shape, BF16)


def _moe_combine_kernel(a_ref, yh_hbm, yl_hbm, x_ref, gf_ref, g_ref, rl_ref, o_ref, ybuf, sem, *, tm, n_e, tile0):
    i = pl.program_id(0)
    n = pl.num_programs(0)
    slot = i % 2

    def copies(step, sl):
        out = []
        for e in range(n_e):
            rows = pl.ds(pl.multiple_of(a_ref[(tile0 + step) * n_e + e], MOE_ALIGN), tm)
            out.append(pltpu.make_async_copy(yh_hbm.at[rows], ybuf.at[sl, 0, e], sem.at[sl]))
            out.append(pltpu.make_async_copy(yl_hbm.at[rows], ybuf.at[sl, 1, e], sem.at[sl]))
        return out

    @pl.when(i == 0)
    def _():
        for cp in copies(0, 0):
            cp.start()

    @pl.when(i + 1 < n)
    def _():
        for cp in copies(i + 1, 1 - slot):
            cp.start()

    for cp in copies(i, slot):
        cp.wait()

    lio = lax.broadcasted_iota(jnp.int32, (tm, tm), 1)
    acc = jnp.zeros(o_ref.shape, F32)
    for e in range(n_e):
        p = jnp.where(lio == rl_ref[:, e:e + 1], 1.0, 0.0).astype(BF16)
        z = _dot(p, ybuf[slot, 0, e]) + _dot(p, ybuf[slot, 1, e])
        acc = acc + g_ref[:, e:e + 1] * z
    o_ref[...] = x_ref[...] + gf_ref[...] * acc


def _ffn_moe_routed(h2s, xs, gfs, gates, w1b, w3b, w2b, *, tf):
    sizes = [x.shape[0] for x in xs]
    t = sum(sizes)
    d = xs[0].shape[1]
    n_e, _, dff = w1b.shape
    tile = MOE_TILE
    tm = next(c for c in (256, 128, 64, 32, 16) if all(sz % c == 0 for sz in sizes))
    nt = t // tm
    ceil_to = lambda v, m: ((v + m - 1) // m) * m
    n_tiles = (2 * t + nt * n_e * (MOE_ALIGN - 1)) // tile + 2 * n_e + 1
    s_pad = n_tiles * tile

    mask = gates > 0.0
    m3 = mask.reshape(nt, tm, n_e).astype(jnp.int32)
    rl = jnp.where(mask, (jnp.cumsum(m3, axis=1) - 1).reshape(t, n_e), -1)
    p_te = ceil_to(jnp.sum(m3, axis=1), MOE_ALIGN)
    act_rows = ceil_to(jnp.sum(p_te, axis=0), tile)
    seg = act_rows + tile
    seg_start = jnp.cumsum(seg) - seg
    a_te = (seg_start[None, :] + jnp.cumsum(p_te, axis=0) - p_te).astype(jnp.int32).reshape(nt * n_e)
    tile_start = jnp.arange(n_tiles, dtype=jnp.int32) * tile
    tile_exp = jnp.minimum(jnp.sum((tile_start[:, None] >= (seg_start + seg)[None, :]).astype(jnp.int32), axis=1),
                           n_e - 1)
    in_seg = tile_start - seg_start[tile_exp]
    tile_act = ((in_seg >= 0) & (in_seg < act_rows[tile_exp])).astype(jnp.int32)
    rl_t = rl.T

    slots = jnp.zeros((s_pad, d), BF16)
    tile0 = 0
    for h2, sz in zip(h2s, sizes):
        slots = pl.pallas_call(
            functools.partial(_moe_dispatch_kernel, tm=tm, n_e=n_e, tile0=tile0),
            grid_spec=pltpu.PrefetchScalarGridSpec(
                num_scalar_prefetch=1, grid=(sz // tm,),
                in_specs=[pl.BlockSpec((tm, d), lambda i, a: (i, 0)),
                          pl.BlockSpec((n_e, tm), lambda i, a, tile0=tile0: (0, tile0 + i)),
                          pl.BlockSpec(memory_space=pl.ANY)],
                out_specs=pl.BlockSpec(memory_space=pl.ANY),
                scratch_shapes=[pltpu.VMEM((2, n_e, tm, d), BF16), pltpu.SemaphoreType.DMA((2,))]),
            out_shape=jax.ShapeDtypeStruct((s_pad, d), BF16),
            input_output_aliases={3: 0},
            compiler_params=_cparams(("arbitrary",)),
            name="moe_dispatch",
        )(a_te, h2, rl_t, slots)
        tile0 += sz // tm

    wspec = lambda shp: pl.BlockSpec((None,) + shp, lambda i, ex, ac: (ex[i], 0, 0))
    slot_spec = pl.BlockSpec((tile, d), lambda i, ex, ac: (i, 0))
    y_hi, y_lo = pl.pallas_call(
        functools.partial(_moe_expert_kernel, tf=tf),
        grid_spec=pltpu.PrefetchScalarGridSpec(
            num_scalar_prefetch=2, grid=(n_tiles,),
            in_specs=[slot_spec, wspec((d, dff)), wspec((d, dff)), wspec((dff, d))],
            out_specs=(slot_spec, slot_spec),
            scratch_shapes=[pltpu.VMEM((tile, d), F32)]),
        out_shape=(jax.ShapeDtypeStruct((s_pad, d), BF16), jax.ShapeDtypeStruct((s_pad, d), BF16)),
        compiler_params=_cparams(("arbitrary",)),
        name="moe_experts",
    )(tile_exp, tile_act, slots, w1b, w3b, w2b)

    outs = []
    tile0 = 0
    for x, gf, sz in zip(xs, gfs, sizes):
        tok = lambda w: pl.BlockSpec((tm, w), lambda i, a: (i, 0))
        tok_g = lambda w, tile0=tile0: pl.BlockSpec((tm, w), lambda i, a: (tile0 + i, 0))
        if gf.ndim == 3:
            per = (sz // gf.shape[0]) // tm
            gf_spec = pl.BlockSpec((None, 1, d), lambda i, a, per=per: (i // per, 0, 0))
        else:
            gf_spec = tok(d)
        outs.append(pl.pallas_call(
            functools.partial(_moe_combine_kernel, tm=tm, n_e=n_e, tile0=tile0),
            grid_spec=pltpu.PrefetchScalarGridSpec(
                num_scalar_prefetch=1, grid=(sz // tm,),
                in_specs=[pl.BlockSpec(memory_space=pl.ANY), pl.BlockSpec(memory_space=pl.ANY),
                          tok(d), gf_spec, tok_g(n_e), tok_g(n_e)],
                out_specs=tok(d),
                scratch_shapes=[pltpu.VMEM((2, 2, n_e, tm, d), BF16), pltpu.SemaphoreType.DMA((2,))]),
            out_shape=jax.ShapeDtypeStruct((sz, d), F32),
            compiler_params=_cparams(("arbitrary",)),
            name="moe_combine",
        )(a_te, y_hi, y_lo, x, gf, gates, rl))
        tile0 += sz // tm
    return outs


def kernel(x_prompt, x_sample, cache_k, cache_v, state_conv, page_table, c_prompt, c_sample, w_mod, b_mod,
           norm1_g, norm2_g, w_in, q_norm_g, k_norm_g, w_o_attn, w_dw, b_dw, conv_ln_g, conv_ln_b, w_pw_conv,
           w_out, ffn_w1, ffn_w3, ffn_w2, router_w, router_b, moe_w1, moe_w3, moe_w2):
    depth = w_in.shape[0]
    n_p, t_p, d = x_prompt.shape
    n_s, t_s, _ = x_sample.shape
    n_tok_s = n_s * t_s
    n_phys = cache_k.shape[1]
    n_blk_p = t_p // MOBA_BLOCK

    c_all = jnp.concatenate([c_prompt, c_sample], axis=0)
    c_all = jnp.pad(c_all, ((0, (-c_all.shape[0]) % 8), (0, 0)))
    mod = _modulation(c_all, w_mod, b_mod)

    cache_kt = cache_k.transpose(0, 1, 3, 4, 2).reshape(depth, n_phys, D_ATTN, PAGE_SIZE)
    cache_vt = cache_v.transpose(0, 1, 3, 4, 2).reshape(depth, n_phys, D_ATTN, PAGE_SIZE)

    head_id = jnp.arange(D_ATTN) // HEAD_DIM
    head_mean = jnp.where(head_id[:, None] == head_id[None, :], 1.0 / HEAD_DIM, 0.0).astype(BF16)

    xp = x_prompt
    xs = x_sample.reshape(1, n_tok_s, d)
    k_p, v_p, conv_p, k_s, v_s, conv_s = [], [], [], [], [], []
    for l in range(depth):
        mod_p = mod[l, :n_p].reshape(n_p, N_MOD, 1, d)
        mod_s = jnp.repeat(mod[l, n_p:n_p + n_s].reshape(n_s, N_MOD, d), t_s, axis=0)
        mp = [mod_p[:, i] for i in range(N_MOD)]
        msm = [mod_s[None, :, i] for i in range(N_MOD)]

        w_in_b = w_in[l].astype(BF16)
        g1 = norm1_g[l].reshape(1, d)
        g2 = norm2_g[l].reshape(1, d)
        gq = jnp.tile(q_norm_g[l], N_HEADS).reshape(1, D_ATTN)
        gk = jnp.tile(k_norm_g[l], N_HEADS).reshape(1, D_ATTN)
        w_o_b = w_o_attn[l].astype(BF16)
        w_pw_b = w_pw_conv[l].astype(BF16)
        w_out_b = w_out[l].astype(BF16)
        ln_g = conv_ln_g[l].reshape(1, D_CONV)
        ln_b = conv_ln_b[l].reshape(1, D_CONV)
        is_moe = l % 2 == 1
        router = None
        if is_moe:
            rw = router_w[l // 2]
            rw_hi = rw.astype(BF16)
            rw_lo = (rw - rw_hi.astype(F32)).astype(BF16)
            router = (rw_hi, rw_lo, router_b[l // 2].reshape(1, N_EXPERTS))

        q_pad, k_aug, v_aug, kf, vf, u, ga, gc, km = _in_projection(
            xp, mp[0], mp[1], g1, w_in_b, head_mean, gq, gk, tm=MOBA_BLOCK, head_major=True)
        kmean = km.reshape(n_p, n_blk_p, N_HEADS, HEAD_DIM).transpose(0, 2, 1, 3)
        kmean = jnp.pad(kmean, ((0, 0), (0, 0), (0, 0), (0, LANES - HEAD_DIM)))
        attn = _moba_prompt(q_pad, k_aug, v_aug, kmean)
        outs = _post_mixer(xp, u, attn, ga, gc, mp[2], mp[3], mp[4], (w_dw[l], b_dw[l].reshape(1, D_CONV)),
                           ln_g, ln_b, w_pw_b, w_o_b, w_out_b, g2, router, tm=256)
        k_p.append(kf.reshape(n_p, t_p, N_HEADS, HEAD_DIM))
        v_p.append(vf.reshape(n_p, t_p, N_HEADS, HEAD_DIM))
        conv_p.append(u[:, t_p - (CONV_W - 1):])
        xp_mid, h2_p = outs[0], outs[1]
        gates_p = outs[2] if is_moe else None

        qs, kfs, vfs, us, gas, gcs = _in_projection(
            xs, msm[0], msm[1], g1, w_in_b, head_mean, gq, gk, tm=min(256, n_tok_s), head_major=False)
        seq = lambda a: a.reshape(n_s, t_s, a.shape[-1])
        attn_s = _moba_sample(page_table, seq(qs), seq(kfs), seq(vfs), cache_kt, cache_vt, l)
        attn_s = attn_s.reshape(1, n_tok_s, D_ATTN).astype(BF16)
        u_s = seq(us)
        y_dw = _conv_sample(state_conv[l], u_s, w_dw[l], b_dw[l]).reshape(1, n_tok_s, D_CONV)
        outs = _post_mixer(xs, y_dw, attn_s, gas, gcs, msm[2], msm[3], msm[4], None,
                           ln_g, ln_b, w_pw_b, w_o_b, w_out_b, g2, router, tm=min(256, n_tok_s))
        k_s.append(kfs.reshape(n_s, t_s, N_HEADS, HEAD_DIM))
        v_s.append(vfs.reshape(n_s, t_s, N_HEADS, HEAD_DIM))
        conv_s.append(jnp.concatenate([state_conv[l], u_s], axis=1)[:, -(CONV_W - 1):])
        xs_mid, h2_s = outs[0], outs[1]
        gates_s = outs[2] if is_moe else None

        if is_moe:
            w1b = moe_w1[l // 2].astype(BF16)
            w3b = moe_w3[l // 2].astype(BF16)
            w2b = moe_w2[l // 2].astype(BF16)
            n_tok_p = n_p * t_p
            gates_all = jnp.concatenate([gates_p.reshape(n_tok_p, N_EXPERTS), gates_s.reshape(n_tok_s, N_EXPERTS)])
            xp, xs = _ffn_moe_routed(
                [h2_p.reshape(n_tok_p, d), h2_s.reshape(n_tok_s, d)],
                [xp_mid.reshape(n_tok_p, d), xs_mid.reshape(n_tok_s, d)],
                [mp[5], msm[5].reshape(n_tok_s, d)], gates_all, w1b, w3b, w2b, tf=1408)
            xp = xp.reshape(n_p, t_p, d)
            xs = xs.reshape(1, n_tok_s, d)
        else:
            w1b = ffn_w1[l // 2].astype(BF16)
            w3b = ffn_w3[l // 2].astype(BF16)
            w2b = ffn_w2[l // 2].astype(BF16)
            xp = _ffn_dense(h2_p, xp_mid, mp[5], w1b, w3b, w2b, tm=512, tf=1408)
            xs = _ffn_dense(h2_s, xs_mid, msm[5], w1b, w3b, w2b, tm=512, tf=1408)

    return (xp, xs.reshape(n_s, t_s, d), jnp.stack(k_p), jnp.stack(v_p), jnp.stack(conv_p),
            jnp.stack(k_s), jnp.stack(v_s), jnp.stack(conv_s))
```
